```python
import jax
import jax.numpy as jnp
from jax import lax
import numpy as np

D_MODEL = 1024
BATCH = 2
SEQ = 8192
DEPTH = 2
DEC_BATCH = 8
DEC_SEQ = 16
PAST_LEN = 4096

CHUNK = 64
LEFT_CHUNKS = 8
ATT_WINDOW = LEFT_CHUNKS * CHUNK
BAND = (LEFT_CHUNKS + 1) * CHUNK
N_HEADS_A = 8
HEAD_DIM_A = 64
WIDTH_A = N_HEADS_A * HEAD_DIM_A
REL_CLIP = 256
N_HEADS_B = 4
HEAD_DIM_B = 128
WIDTH_B = N_HEADS_B * HEAD_DIM_B
MIX_WIDTH = WIDTH_A + WIDTH_B
D_FF = 2816
CONV_W = 3
EPS = 1e-6
IN_SPLITS = (WIDTH_A, 2 * WIDTH_A, 3 * WIDTH_A,
             3 * WIDTH_A + WIDTH_B, 3 * WIDTH_A + 2 * WIDTH_B, 3 * WIDTH_A + 3 * WIDTH_B,
             3 * WIDTH_A + 4 * WIDTH_B, 3 * WIDTH_A + 4 * WIDTH_B + N_HEADS_B)
D_IN = 3 * WIDTH_A + 4 * WIDTH_B + 2 * N_HEADS_B

kernel_name = "hymba_chunkattn_mlstm_convffn_step"


def rms_norm(x, g):
    xf = x.astype(jnp.float32)
    y = xf * lax.rsqrt(jnp.mean(xf * xf, axis=-1, keepdims=True) + EPS)
    return (y * g.astype(jnp.float32)).astype(x.dtype)


def rel_bias(rel_table, q_pos, k_pos):
    idx = jnp.clip(q_pos[:, None] - k_pos[None, :], -REL_CLIP, REL_CLIP) + REL_CLIP
    return rel_table[:, idx].astype(jnp.float32)


def band_attention_prompt(q, k, v, rel_table):
    B, S = q.shape[:2]
    n_chunks = S // CHUNK
    qc = q.reshape(B, n_chunks, CHUNK, N_HEADS_A, HEAD_DIM_A)
    pad = ((0, 0), (ATT_WINDOW, 0), (0, 0), (0, 0))
    kp = jnp.pad(k, pad).reshape(B, n_chunks + LEFT_CHUNKS, CHUNK, N_HEADS_A, HEAD_DIM_A)
    vp = jnp.pad(v, pad).reshape(B, n_chunks + LEFT_CHUNKS, CHUNK, N_HEADS_A, HEAD_DIM_A)
    band_idx = jnp.arange(n_chunks)[:, None] + jnp.arange(LEFT_CHUNKS + 1)[None, :]
    kb = kp[:, band_idx].reshape(B, n_chunks, BAND, N_HEADS_A, HEAD_DIM_A)
    vb = vp[:, band_idx].reshape(B, n_chunks, BAND, N_HEADS_A, HEAD_DIM_A)
    s = jnp.einsum('bcqhd,bckhd->bhcqk', qc, kb, preferred_element_type=jnp.float32)
    s = s * (HEAD_DIM_A ** -0.5)
    bias = rel_bias(rel_table, ATT_WINDOW + jnp.arange(CHUNK), jnp.arange(BAND))
    s = s + bias[None, :, None]
    valid = (jnp.arange(n_chunks)[:, None] + jnp.arange(BAND)[None, :] // CHUNK) >= LEFT_CHUNKS
    s = jnp.where(valid[None, None, :, None, :], s, -jnp.inf)
    p = jax.nn.softmax(s, axis=-1).astype(v.dtype)
    o = jnp.einsum('bhcqk,bckhd->bcqhd', p, vb)
    return o.reshape(B, S, WIDTH_A)


def band_attention_step(q, k, v, k_cache, v_cache, rel_table):
    B, S = q.shape[:2]
    L = k_cache.shape[2]
    kk = jnp.concatenate([k_cache.astype(k.dtype), k.transpose(0, 2, 1, 3)], axis=2)
    vv = jnp.concatenate([v_cache.astype(v.dtype), v.transpose(0, 2, 1, 3)], axis=2)
    s = jnp.einsum('bqhd,bhkd->bhqk', q, kk, preferred_element_type=jnp.float32)
    s = s * (HEAD_DIM_A ** -0.5) + rel_bias(rel_table, L + jnp.arange(S), jnp.arange(L + S))[None]
    p = jax.nn.softmax(s, axis=-1).astype(v.dtype)
    o = jnp.einsum('bhqk,bhkd->bqhd', p, vv)
    return o.reshape(B, S, WIDTH_A)


def mlstm_block(carry, inp):
    c_prev, n_prev, m_prev = carry
    q, k, v, ig, lf = inp
    L = q.shape[2]
    b = jnp.cumsum(lf, axis=-1)
    causal = jnp.tril(jnp.ones((L, L), dtype=bool))
    d_mat = b[..., :, None] - b[..., None, :] + ig[..., None, :]
    d_mat = jnp.where(causal, d_mat, -jnp.inf)
    inter = b + m_prev[..., None]
    m_t = jnp.maximum(inter, jnp.max(d_mat, axis=-1))
    w_intra = jnp.exp(d_mat - m_t[..., None])
    w_inter = jnp.exp(inter - m_t)
    qk = jnp.einsum('bhtd,bhsd->bhts', q, k) * w_intra
    num = jnp.einsum('bhts,bhse->bhte', qk, v) + w_inter[..., None] * jnp.einsum('bhtd,bhde->bhte', q, c_prev)
    den = jnp.sum(qk, axis=-1) + w_inter * jnp.einsum('bhtd,bhd->bht', q, n_prev)
    h = num / jnp.maximum(jnp.abs(den), jnp.exp(-m_t))[..., None]
    m_new = m_t[..., -1]
    w_s = jnp.exp(b[..., -1:] - b + ig - m_new[..., None])
    decay = jnp.exp(b[..., -1] + m_prev - m_new)
    c_new = decay[..., None, None] * c_prev + jnp.einsum('bhs,bhsd,bhse->bhde', w_s, k, v)
    n_new = decay[..., None] * n_prev + jnp.einsum('bhs,bhsd->bhd', w_s, k)
    return (c_new, n_new, m_new), h


def mlstm_sequence(q, k, v, ig, lf, c0, n0, m0):
    B, H, S, d = q.shape
    carry0 = (c0.astype(jnp.float32), n0.astype(jnp.float32), m0.astype(jnp.float32))
    if S <= CHUNK:
        carry, h = mlstm_block(carry0, (q, k, v, ig, lf))
        return h, carry
    nb = S // CHUNK

    def to_blocks(t):
        return jnp.moveaxis(t.reshape((B, H, nb, CHUNK) + t.shape[3:]), 2, 0)

    xs = (to_blocks(q), to_blocks(k), to_blocks(v), to_blocks(ig), to_blocks(lf))
    carry, hb = lax.scan(mlstm_block, carry0, xs)
    h = jnp.moveaxis(hb, 0, 2).reshape(B, H, S, d)
    return h, carry


def conv_ffn(h, buf, w_up, w_conv, b_conv, w_down):
    S = h.shape[1]
    u = h @ w_up
    ext = jnp.concatenate([buf.astype(u.dtype), u], axis=1)
    y = b_conv
    for j in range(CONV_W):
        y = y + ext[:, j:j + S] * w_conv[j]
    gate, up = jnp.split(y, 2, axis=-1)
    out = (jax.nn.gelu(gate, approximate=True) * up) @ w_down
    return out, ext[:, S:]


def layer(x, k_cache, v_cache, c0, n0, m0, conv_buf, norm_g, w_in, b_i, b_f, rel_table,
          g_att, g_mlstm, w_out, w_up, w_conv, b_conv, w_down):
    B, S, _ = x.shape
    h = rms_norm(x, norm_g[0])
    z = h @ w_in
    qa, ka, va, qb, kb, vb, ob, ib, fb = jnp.split(z, IN_SPLITS, axis=-1)
    qa = qa.reshape(B, S, N_HEADS_A, HEAD_DIM_A)
    ka = ka.reshape(B, S, N_HEADS_A, HEAD_DIM_A)
    va = va.reshape(B, S, N_HEADS_A, HEAD_DIM_A)
    if k_cache is None:
        att = band_attention_prompt(qa, ka, va, rel_table)
        keep = min(ATT_WINDOW, S)
        new_k = ka[:, S - keep:].transpose(0, 2, 1, 3)
        new_v = va[:, S - keep:].transpose(0, 2, 1, 3)
    else:
        att = band_attention_step(qa, ka, va, k_cache, v_cache, rel_table)
        new_k = ka.transpose(0, 2, 1, 3)
        new_v = va.transpose(0, 2, 1, 3)
    att = rms_norm(att, g_att)

    def heads_b(t):
        return t.reshape(B, S, N_HEADS_B, HEAD_DIM_B).transpose(0, 2, 1, 3).astype(jnp.float32)

    qh = heads_b(qb)
    kh = heads_b(kb) * (HEAD_DIM_B ** -0.5)
    vh = heads_b(vb)
    ig = (ib + b_i).astype(jnp.float32).transpose(0, 2, 1)
    lf = jax.nn.log_sigmoid((fb + b_f).astype(jnp.float32)).transpose(0, 2, 1)
    hb, (c1, n1, m1) = mlstm_sequence(qh, kh, vh, ig, lf, c0, n0, m0)
    hb = hb * lax.rsqrt(jnp.mean(hb * hb, axis=-1, keepdims=True) + EPS)
    hb = hb.transpose(0, 2, 1, 3).reshape(B, S, WIDTH_B)
    mlstm_out = (hb * g_mlstm.astype(jnp.float32) * jax.nn.sigmoid(ob.astype(jnp.float32))).astype(x.dtype)

    mix = jnp.concatenate([att, mlstm_out], axis=-1) @ w_out
    x = x + rms_norm(mix, norm_g[1])
    ffn, new_buf = conv_ffn(rms_norm(x, norm_g[2]), conv_buf, w_up, w_conv, b_conv, w_down)
    x = x + rms_norm(ffn, norm_g[3])
    return x, new_k, new_v, c1, n1, m1, new_buf


def setup_inputs(seed: int = 0) -> dict:
    key = jax.random.key(seed)
    ks = jax.random.split(key, 20)
    att_cache = min(ATT_WINDOW, PAST_LEN)
    f32 = jnp.float32
    nrm = lambda k, shape, s: (jax.random.normal(k, shape, f32) * s)
    return {
        "x_prompt": nrm(ks[0], (BATCH, SEQ, D_MODEL), 1.0),
        "x_sample": nrm(ks[1], (DEC_BATCH, DEC_SEQ, D_MODEL), 1.0),
        "cache_k_att": nrm(ks[2], (DEPTH, DEC_BATCH, N_HEADS_A, att_cache, HEAD_DIM_A), 1.0),
        "cache_v_att": nrm(ks[3], (DEPTH, DEC_BATCH, N_HEADS_A, att_cache, HEAD_DIM_A), 1.0),
        "state_mlstm_c": nrm(ks[4], (DEPTH, DEC_BATCH, N_HEADS_B, HEAD_DIM_B, HEAD_DIM_B), 0.1),
        "state_mlstm_n": nrm(ks[5], (DEPTH, DEC_BATCH, N_HEADS_B, HEAD_DIM_B), 0.1),
        "state_mlstm_m": nrm(ks[6], (DEPTH, DEC_BATCH, N_HEADS_B), 0.5),
        "cache_ffn_conv": nrm(ks[7], (DEPTH, DEC_BATCH, CONV_W - 1, 2 * D_FF), 1.0),
        "norm_g": 1.0 + nrm(ks[8], (DEPTH, 4, D_MODEL), 0.05),
        "w_in": nrm(ks[9], (DEPTH, D_MODEL, D_IN), D_MODEL ** -0.5),
        "b_i": nrm(ks[10], (DEPTH, N_HEADS_B), 0.1),
        "b_f": jnp.linspace(3.0, 6.0, N_HEADS_B, dtype=f32)[None, :] + nrm(ks[11], (DEPTH, N_HEADS_B), 0.1),
        "rel_table": nrm(ks[12], (DEPTH, N_HEADS_A, 2 * REL_CLIP + 1), 0.5),
        "g_att": 1.0 + nrm(ks[13], (DEPTH, WIDTH_A), 0.05),
        "g_mlstm": 1.0 + nrm(ks[14], (DEPTH, WIDTH_B), 0.05),
        "w_out": nrm(ks[15], (DEPTH, MIX_WIDTH, D_MODEL), MIX_WIDTH ** -0.5),
        "w_up": nrm(ks[16], (DEPTH, D_MODEL, 2 * D_FF), D_MODEL ** -0.5),
        "w_conv": nrm(ks[17], (DEPTH, CONV_W, 2 * D_FF), CONV_W ** -0.5),
        "b_conv": nrm(ks[18], (DEPTH, 2 * D_FF), 0.02),
        "w_down": nrm(ks[19], (DEPTH, D_FF, D_MODEL), D_FF ** -0.5),
    }


def reference(x_prompt, x_sample, cache_k_att, cache_v_att, state_mlstm_c, state_mlstm_n,
              state_mlstm_m, cache_ffn_conv, norm_g, w_in, b_i, b_f, rel_table, g_att, g_mlstm,
              w_out, w_up, w_conv, b_conv, w_down):
    bp = x_prompt.shape[0]
    c_zero = jnp.zeros((bp, N_HEADS_B, HEAD_DIM_B, HEAD_DIM_B), jnp.float32)
    n_zero = jnp.zeros((bp, N_HEADS_B, HEAD_DIM_B), jnp.float32)
    m_zero = jnp.zeros((bp, N_HEADS_B), jnp.float32)
    buf_zero = jnp.zeros((bp, CONV_W - 1, 2 * D_FF), x_prompt.dtype)
    xp = x_prompt
    xs = x_sample
    pk, pv, pc, pn, pm, pconv = [], [], [], [], [], []
    sk, sv, sc, sn, sm, sconv = [], [], [], [], [], []
    for l in range(DEPTH):
        xp, k1, v1, c1, n1, m1, b1 = layer(
            xp, None, None, c_zero, n_zero, m_zero, buf_zero, norm_g[l], w_in[l], b_i[l], b_f[l],
            rel_table[l], g_att[l], g_mlstm[l], w_out[l], w_up[l], w_conv[l], b_conv[l], w_down[l])
        pk.append(k1); pv.append(v1); pc.append(c1); pn.append(n1); pm.append(m1); pconv.append(b1)
        xs, k2, v2, c2, n2, m2, b2 = layer(
            xs, cache_k_att[l], cache_v_att[l], state_mlstm_c[l], state_mlstm_n[l], state_mlstm_m[l],
            cache_ffn_conv[l], norm_g[l], w_in[l], b_i[l], b_f[l], rel_table[l], g_att[l], g_mlstm[l],
            w_out[l], w_up[l], w_conv[l], b_conv[l], w_down[l])
        sk.append(k2); sv.append(v2); sc.append(c2); sn.append(n2); sm.append(m2); sconv.append(b2)
    k_att_prompt = jnp.stack(pk)
    v_att_prompt = jnp.stack(pv)
    mlstm_c_prompt = jnp.stack(pc)
    mlstm_n_prompt = jnp.stack(pn)
    mlstm_m_prompt = jnp.stack(pm)
    ffn_conv_prompt = jnp.stack(pconv)
    k_att_sample = jnp.stack(sk)
    v_att_sample = jnp.stack(sv)
    mlstm_c_sample = jnp.stack(sc)
    mlstm_n_sample = jnp.stack(sn)
    mlstm_m_sample = jnp.stack(sm)
    ffn_conv_sample = jnp.stack(sconv)
    return (xp, xs, k_att_prompt, v_att_prompt, mlstm_c_prompt, mlstm_n_prompt, mlstm_m_prompt,
            ffn_conv_prompt, k_att_sample, v_att_sample, mlstm_c_sample, mlstm_n_sample,
            mlstm_m_sample, ffn_conv_sample)
```

```python
import functools

import jax
import jax.numpy as jnp
from jax import lax
from jax.experimental import pallas as pl
from jax.experimental.pallas import tpu as pltpu

F32 = jnp.float32
BF16 = jnp.bfloat16

CHUNK = 64
LEFT_CHUNKS = 8
ATT_WINDOW = LEFT_CHUNKS * CHUNK
REL_CLIP = 256
N_HEADS_A = 8
HEAD_DIM_A = 64
WIDTH_A = N_HEADS_A * HEAD_DIM_A
N_HEADS_B = 4
HEAD_DIM_B = 128
WIDTH_B = N_HEADS_B * HEAD_DIM_B
CONV_W = 3
EPS = 1e-6
NEG = -1e30

LANES = 128
VMEM_LIMIT_BYTES = 56 * 1024 * 1024

INPROJ_ROWS = 512
ATT_ROWS = ATT_WINDOW
ATT_SUB = 2 * CHUNK
ATT_BAND = ATT_SUB + ATT_WINDOW
BIAS_W = 768
MLSTM_BLOCK = 128
FFN_ROWS = 256
FFN_COLS = 256


def _params(n_axes):
    return pltpu.CompilerParams(dimension_semantics=("arbitrary",) * n_axes,
                                vmem_limit_bytes=VMEM_LIMIT_BYTES)


def _const_spec(shape):
    nd = len(shape)
    return pl.BlockSpec(shape, lambda *_: (0,) * nd, pipeline_mode=pl.Buffered(1))


def _rms(x):
    return x * lax.rsqrt(jnp.mean(x * x, axis=-1, keepdims=True) + EPS)


def _dot(a, b):
    return jnp.dot(a, b, preferred_element_type=F32)


def _dot_nt(a, b):
    return lax.dot_general(a, b, (((1,), (1,)), ((), ())), preferred_element_type=F32)


def _inproj_body(x_ref, g_ref, wm_ref, wkt_ref, wg_ref,
                 qa_ref, ka_ref, va_ref, qb_ref, vb_ref, ob_ref, kbt_ref, gate_ref):
    h = _rms(x_ref[...]) * g_ref[...]
    hb = h.astype(BF16)
    outs = (qa_ref, ka_ref, va_ref, qb_ref, vb_ref, ob_ref)
    for c, o_ref in enumerate(outs):
        z = _dot(hb, wm_ref[:, c * WIDTH_A:(c + 1) * WIDTH_A])
        if c == 0:
            z = z * (HEAD_DIM_A ** -0.5)
        o_ref[...] = z.astype(BF16)
    zk = _dot_nt(wkt_ref[...], hb)
    kbt_ref[...] = (zk * (HEAD_DIM_B ** -0.5)).astype(BF16)
    lane = lax.broadcasted_iota(jnp.int32, gate_ref.shape, 1)
    acc = jnp.zeros(gate_ref.shape, F32)
    for g in range(2 * N_HEADS_B):
        col = jnp.sum(h * wg_ref[g:g + 1, :], axis=-1, keepdims=True)
        acc = jnp.where(lane == g, col, acc)
    gate_ref[...] = acc


def _inproj(x2d, g, wm, wkt, wg):
    t, d = x2d.shape
    tm = min(INPROJ_ROWS, t)
    assert t % tm == 0
    row_spec = lambda w: pl.BlockSpec((tm, w), lambda i: (i, 0))
    out_shape = [jax.ShapeDtypeStruct((t, WIDTH_A), BF16)] * 6 + [
        jax.ShapeDtypeStruct((WIDTH_B, t), BF16), jax.ShapeDtypeStruct((t, LANES), F32)]
    out_specs = [row_spec(WIDTH_A)] * 6 + [pl.BlockSpec((WIDTH_B, tm), lambda i: (0, i)), row_spec(LANES)]
    return pl.pallas_call(
        _inproj_body, grid=(t // tm,),
        in_specs=[row_spec(d), _const_spec(g.shape), _const_spec(wm.shape), _const_spec(wkt.shape),
                  _const_spec(wg.shape)],
        out_specs=out_specs, out_shape=out_shape, compiler_params=_params(1), name="inproj",
    )(x2d, g, wm, wkt, wg)


def _gates_body(gi_ref, gf_ref, bi_ref, bf_ref, ig_ref, bc_ref, *, n_valid):
    rows, blk = gi_ref.shape
    lane = lax.broadcasted_iota(jnp.int32, (rows, blk), 1)
    ig = gi_ref[...] + bi_ref[...]
    f = gf_ref[...] + bf_ref[...]
    lf = jnp.minimum(f, 0.0) - jnp.log(1.0 + jnp.exp(-jnp.abs(f)))
    if n_valid < blk:
        ig = jnp.where(lane < n_valid, ig, NEG)
        lf = jnp.where(lane < n_valid, lf, 0.0)
    b = lf
    k = 1
    while k < blk:
        b = b + jnp.where(lane >= k, pltpu.roll(b, k, 1), 0.0)
        k *= 2
    ig_ref[...] = ig
    bc_ref[...] = b


def _gates(gi, gf, bi_rows, bf_rows, n_valid):
    rows, blk = gi.shape
    full = lambda a: pl.BlockSpec(a.shape, lambda i: (0, 0))
    return pl.pallas_call(
        functools.partial(_gates_body, n_valid=n_valid), grid=(1,),
        in_specs=[full(gi), full(gf), full(bi_rows), full(bf_rows)],
        out_specs=[full(gi), full(gi)],
        out_shape=[jax.ShapeDtypeStruct((rows, blk), F32)] * 2,
        compiler_params=_params(1), name="gates",
    )(gi, gf, bi_rows, bf_rows)


def _mlstm_body(q_ref, kt_ref, v_ref, o_ref, ig_ref, bc_ref, g_ref, c0_ref, m0_ref,
                h_ref, c_out_ref, m_out_ref, c_sc, m_sc, vext_sc):
    j = pl.program_id(1)
    blk = q_ref.shape[0]
    dh = HEAD_DIM_B

    @pl.when(j == 0)
    def _load_state():
        c_sc[...] = c0_ref[...]
        m_sc[...] = m0_ref[...]
        lane = lax.broadcasted_iota(jnp.int32, (blk, dh), 1)
        ones_col = jnp.where(lane == 0, 1.0, 0.0).astype(BF16)
        for hd in range(N_HEADS_B):
            vext_sc[hd, :, dh:] = ones_col

    row = lax.broadcasted_iota(jnp.int32, (blk, blk), 0)
    col = lax.broadcasted_iota(jnp.int32, (blk, blk), 1)
    for hd in range(N_HEADS_B):
        sl = slice(hd * dh, (hd + 1) * dh)
        q = q_ref[:, sl]
        kt = kt_ref[sl, :]
        vext_sc[hd, :, :dh] = v_ref[:, sl]
        vext = vext_sc[hd]
        ig_r = ig_ref[hd:hd + 1, :]
        b_r = bc_ref[hd:hd + 1, :]
        m_prev = m_sc[hd:hd + 1, 0:1]
        b_c = jnp.sum(jnp.where(row == col, b_r, 0.0), axis=1, keepdims=True)
        dmat = jnp.where(col <= row, b_c - b_r + ig_r, NEG)
        inter = b_c + m_prev
        m_t = jnp.maximum(inter, jnp.max(dmat, axis=1, keepdims=True))
        w_intra = jnp.exp(dmat - m_t)
        w_inter = jnp.exp(inter - m_t)
        s = (_dot(q, kt) * w_intra).astype(BF16)
        c_old = c_sc[hd]
        ext = _dot(s, vext) + w_inter * _dot(q, c_old.astype(BF16))
        num = ext[:, :dh]
        den = ext[:, dh:dh + 1]
        hh = num / jnp.maximum(jnp.abs(den), jnp.exp(-m_t))
        hh = _rms(hh)
        gate = 1.0 / (1.0 + jnp.exp(-o_ref[:, sl].astype(F32)))
        h_ref[:, sl] = (hh * g_ref[:, sl] * gate).astype(BF16)
        m_new = m_t[blk - 1:blk, :]
        b_last = b_r[:, blk - 1:blk]
        w_s = jnp.exp(b_last - b_r + ig_r - m_new)
        decay = jnp.exp(b_last + m_prev - m_new)
        ktw = (kt.astype(F32) * w_s).astype(BF16)
        c_sc[hd] = decay * c_old + _dot(ktw, vext)
        m_sc[hd:hd + 1, :] = jnp.broadcast_to(m_new, (1, m_sc.shape[1]))

    @pl.when(j == pl.num_programs(1) - 1)
    def _store_state():
        c_out_ref[...] = c_sc[...]
        m_out_ref[...] = m_sc[...]


def _mlstm(q, kt, v, o, ig, bc, g, c0, m0, blk):
    t = q.shape[0]
    nb = c0.shape[0]
    nblk = t // (nb * blk)
    dh = HEAD_DIM_B
    tok = pl.BlockSpec((blk, WIDTH_B), lambda b, j: (b * nblk + j, 0))
    gspec = pl.BlockSpec((None, N_HEADS_B, blk), lambda b, j: (b * nblk + j, 0, 0))
    cspec = pl.BlockSpec((None, N_HEADS_B, dh, 2 * dh), lambda b, j: (b, 0, 0, 0))
    mspec = pl.BlockSpec((None, N_HEADS_B, LANES), lambda b, j: (b, 0, 0))
    return pl.pallas_call(
        _mlstm_body, grid=(nb, nblk),
        in_specs=[tok, pl.BlockSpec((WIDTH_B, blk), lambda b, j: (0, b * nblk + j)), tok, tok,
                  gspec, gspec, pl.BlockSpec(g.shape, lambda b, j: (0, 0)), cspec, mspec],
        out_specs=[tok, cspec, mspec],
        out_shape=[jax.ShapeDtypeStruct((t, WIDTH_B), BF16),
                   jax.ShapeDtypeStruct(c0.shape, F32), jax.ShapeDtypeStruct(m0.shape, F32)],
        scratch_shapes=[pltpu.VMEM((N_HEADS_B, dh, 2 * dh), F32), pltpu.VMEM((N_HEADS_B, LANES), F32),
                        pltpu.VMEM((N_HEADS_B, blk, 2 * dh), BF16)],
        compiler_params=_params(2), name="mlstm",
    )(q, kt, v, o, ig.reshape(nb * nblk, N_HEADS_B, blk), bc.reshape(nb * nblk, N_HEADS_B, blk), g, c0, m0)


def _toeplitz_bias(gext_row, rows):
    gx = jnp.broadcast_to(gext_row, (rows, BIAS_W))
    return pltpu.roll(gx, 0, 1, stride=1, stride_axis=0)


def _attn_prompt_body(q_ref, kp_ref, kc_ref, vp_ref, vc_ref, gext_ref, gatt_ref, o_ref,
                      kcat, vcat, bias_sc, o_sc):
    b = pl.program_id(0)
    i = pl.program_id(1)

    @pl.when((b == 0) & (i == 0))
    def _init_bias():
        r = lax.broadcasted_iota(jnp.int32, (ATT_SUB, BIAS_W), 0)
        c = lax.broadcasted_iota(jnp.int32, (ATT_SUB, BIAS_W), 1)
        ok = ((r < CHUNK) & (c < ATT_WINDOW + CHUNK)) | ((r >= CHUNK) & (c >= CHUNK))
        for hd in range(N_HEADS_A):
            t = _toeplitz_bias(gext_ref[hd:hd + 1, :], ATT_SUB)
            bias_sc[hd] = jnp.where(ok, t, NEG)[:, :ATT_BAND]

    kcat[:ATT_ROWS] = kp_ref[...]
    kcat[ATT_ROWS:] = kc_ref[...]
    vcat[:ATT_ROWS] = vp_ref[...]
    vcat[ATT_ROWS:] = vc_ref[...]
    ccol = lax.broadcasted_iota(jnp.int32, (ATT_SUB, ATT_BAND), 1)
    for j in range(ATT_ROWS // ATT_SUB):
        r0 = j * ATT_SUB
        live = ccol >= jnp.where(i > 0, 0, ATT_ROWS - r0)
        for hd in range(N_HEADS_A):
            sl = slice(hd * HEAD_DIM_A, (hd + 1) * HEAD_DIM_A)
            s = _dot_nt(q_ref[r0:r0 + ATT_SUB, sl], kcat[r0:r0 + ATT_BAND, sl]) + bias_sc[hd]
            s = jnp.where(live, s, NEG)
            p = jnp.exp(s - jnp.max(s, axis=-1, keepdims=True))
            l = jnp.sum(p, axis=-1, keepdims=True)
            o_sc[:, sl] = _dot(p.astype(BF16), vcat[r0:r0 + ATT_BAND, sl]) / l
        o_ref[r0:r0 + ATT_SUB, :] = (_rms(o_sc[...]) * gatt_ref[...]).astype(BF16)


def _attn_prompt(qa, ka, va, gext, gatt, nb):
    t = qa.shape[0]
    nblk = t // (nb * ATT_ROWS)
    cur = pl.BlockSpec((ATT_ROWS, WIDTH_A), lambda b, i: (b * nblk + i, 0))
    prev = pl.BlockSpec((ATT_ROWS, WIDTH_A), lambda b, i: (b * nblk + jnp.maximum(i - 1, 0), 0))
    return pl.pallas_call(
        _attn_prompt_body, grid=(nb, nblk),
        in_specs=[cur, prev, cur, prev, cur, pl.BlockSpec(gext.shape, lambda b, i: (0, 0)),
                  pl.BlockSpec(gatt.shape, lambda b, i: (0, 0))],
        out_specs=cur, out_shape=jax.ShapeDtypeStruct((t, WIDTH_A), BF16),
        scratch_shapes=[pltpu.VMEM((2 * ATT_ROWS, WIDTH_A), BF16), pltpu.VMEM((2 * ATT_ROWS, WIDTH_A), BF16),
                        pltpu.VMEM((N_HEADS_A, ATT_SUB, ATT_BAND), F32), pltpu.VMEM((ATT_SUB, WIDTH_A), F32)],
        compiler_params=_params(2), name="attn_prompt",
    )(qa, ka, ka, va, va, gext, gatt)


def _attn_step_body(q_ref, kn_ref, vn_ref, ck_ref, cv_ref, gext_ref, gatt_ref, o_ref, o_sc):
    s_new = q_ref.shape[0]
    l_cache = ck_ref.shape[1]
    for hd in range(N_HEADS_A):
        sl = slice(hd * HEAD_DIM_A, (hd + 1) * HEAD_DIM_A)
        bias = _toeplitz_bias(gext_ref[hd:hd + 1, :], s_new)
        q = q_ref[:, sl]
        s1 = _dot_nt(q, ck_ref[hd].astype(BF16)) + bias[:, :l_cache]
        s2 = _dot_nt(q, kn_ref[:, sl]) + bias[:, l_cache:l_cache + s_new]
        m = jnp.maximum(jnp.max(s1, axis=-1, keepdims=True), jnp.max(s2, axis=-1, keepdims=True))
        p1 = jnp.exp(s1 - m)
        p2 = jnp.exp(s2 - m)
        l = jnp.sum(p1, axis=-1, keepdims=True) + jnp.sum(p2, axis=-1, keepdims=True)
        o = _dot(p1.astype(BF16), cv_ref[hd].astype(BF16)) + _dot(p2.astype(BF16), vn_ref[:, sl])
        o_sc[:, sl] = o / l
    o_ref[...] = (_rms(o_sc[...]) * gatt_ref[...]).astype(BF16)


def _attn_step(qa, ka, va, ck, cv, gext, gatt):
    nb, nh, l_cache, dh = ck.shape
    s_new = qa.shape[0] // nb
    tok = pl.BlockSpec((s_new, WIDTH_A), lambda b: (b, 0))
    cache = pl.BlockSpec((None, nh, l_cache, dh), lambda b: (b, 0, 0, 0))
    return pl.pallas_call(
        _attn_step_body, grid=(nb,),
        in_specs=[tok, tok, tok, cache, cache, pl.BlockSpec(gext.shape, lambda b: (0, 0)),
                  pl.BlockSpec(gatt.shape, lambda b: (0, 0))],
        out_specs=tok, out_shape=jax.ShapeDtypeStruct(qa.shape, BF16),
        scratch_shapes=[pltpu.VMEM((s_new, WIDTH_A), F32)],
        compiler_params=_params(1), name="attn_step",
    )(qa, ka, va, ck, cv, gext, gatt)


def _gelu_tanh(x):
    return 0.5 * x * (1.0 + jnp.tanh(0.7978845608028654 * (x + 0.044715 * (x * x * x))))


def _ffn_body(x_ref, att_ref, ml_ref, ng_ref, wo_ref, wu_ref, wc_ref, bc_ref, wd_ref, init_ref,
              y_ref, tail_ref, carry_sc, a_sc, *, seq_rows):
    tm = x_ref.shape[0]
    d_ff = wd_ref.shape[0]
    multi_seq = seq_rows < tm
    mix = _dot(att_ref[...], wo_ref[:WIDTH_A, :]) + _dot(ml_ref[...], wo_ref[WIDTH_A:, :])
    x1 = x_ref[...] + _rms(mix) * ng_ref[1:2, :]
    h2 = (_rms(x1) * ng_ref[2:3, :]).astype(BF16)

    row = lax.broadcasted_iota(jnp.int32, (tm, FFN_COLS), 0)
    if multi_seq:
        pos = row % seq_rows
    else:
        pos = row

        @pl.when(pl.program_id(0) % (seq_rows // tm) == 0)
        def _seq_start():
            carry_sc[...] = init_ref[...]

    def conv(cols):
        u = _dot(h2, wu_ref[:, cols])
        if multi_seq:
            fix = init_ref[:, cols]
            um2 = jnp.where(pos < 2, fix, pltpu.roll(u, 2, 0))
            um1 = jnp.where(pos < 1, pltpu.roll(fix, tm - 1, 0), pltpu.roll(u, 1, 0))
            tail_ref[:, cols] = u
        else:
            c0 = carry_sc[0:1, cols]
            c1 = carry_sc[1:2, cols]
            um2 = jnp.where(pos >= 2, pltpu.roll(u, 2, 0), jnp.where(pos == 0, c0, c1))
            um1 = jnp.where(pos >= 1, pltpu.roll(u, 1, 0), c1)
            carry_sc[:, cols] = u[tm - 2:, :]
            tail_ref[:, cols] = u[tm - 2:, :]
        return bc_ref[:, cols] + um2 * wc_ref[0:1, cols] + um1 * wc_ref[1:2, cols] + u * wc_ref[2:3, cols]

    for c in range(d_ff // FFN_COLS):
        gate = conv(slice(c * FFN_COLS, (c + 1) * FFN_COLS))
        up = conv(slice(d_ff + c * FFN_COLS, d_ff + (c + 1) * FFN_COLS))
        a_sc[:, c * FFN_COLS:(c + 1) * FFN_COLS] = (_gelu_tanh(gate) * up).astype(BF16)
    f = _dot(a_sc[...], wd_ref[...])
    y_ref[...] = x1 + _rms(f) * ng_ref[3:4, :]


def _ffn(x2d, att, ml, ng, wo, wu, wc, bc, wd, init, seq_rows):
    t, d = x2d.shape
    d_ff = wd.shape[0]
    tm = min(FFN_ROWS, t)
    assert t % tm == 0 and d_ff % FFN_COLS == 0
    multi_seq = seq_rows < tm
    if multi_seq:
        assert tm % seq_rows == 0 and t == tm
        init_spec = pl.BlockSpec(init.shape, lambda i: (0, 0))
        tail_spec = pl.BlockSpec((tm, 2 * d_ff), lambda i: (0, 0))
        tail_shape = jax.ShapeDtypeStruct((tm, 2 * d_ff), F32)
    else:
        assert seq_rows % tm == 0
        per_seq = seq_rows // tm
        init_spec = pl.BlockSpec((None, CONV_W - 1, 2 * d_ff), lambda i: (i // per_seq, 0, 0))
        tail_spec = init_spec
        tail_shape = jax.ShapeDtypeStruct(init.shape, F32)
    row_spec = lambda w: pl.BlockSpec((tm, w), lambda i: (i, 0))
    return pl.pallas_call(
        functools.partial(_ffn_body, seq_rows=seq_rows), grid=(t // tm,),
        in_specs=[row_spec(d), row_spec(WIDTH_A), row_spec(WIDTH_B), _const_spec(ng.shape),
                  _const_spec(wo.shape), _const_spec(wu.shape), _const_spec(wc.shape),
                  _const_spec(bc.shape), _const_spec(wd.shape), init_spec],
        out_specs=[row_spec(d), tail_spec],
        out_shape=[jax.ShapeDtypeStruct((t, d), F32), tail_shape],
        scratch_shapes=[pltpu.VMEM((CONV_W - 1, 2 * d_ff), F32), pltpu.VMEM((tm, d_ff), BF16)],
        compiler_params=_params(1), name="ffn",
    )(x2d, att, ml, ng, wo, wu, wc, bc, wd, init)


def _bias_rows(rel_table_l):
    far = jnp.broadcast_to(rel_table_l[:, 2 * REL_CLIP:], (N_HEADS_A, ATT_WINDOW - REL_CLIP + 1))
    near = rel_table_l[:, ATT_SUB + 1:2 * REL_CLIP][:, ::-1]
    wrap = jnp.broadcast_to(rel_table_l[:, 2 * REL_CLIP:], (N_HEADS_A, BIAS_W - ATT_BAND))
    return jnp.concatenate([far, near, wrap], axis=1)


def _gate_rows(gates, nb, seq, blk):
    nblk = -(-seq // blk)
    g = gates[:, :2 * N_HEADS_B].reshape(nb, seq, 2 * N_HEADS_B)
    g = jnp.pad(g, ((0, 0), (0, nblk * blk - seq), (0, 0)))
    g = g.reshape(nb, nblk, blk, 2, N_HEADS_B).transpose(3, 0, 1, 4, 2)
    g = g.reshape(2, nb * nblk * N_HEADS_B, blk)
    return g[0], g[1]


def _pad_tokens(a, nb, seq, blk):
    if seq % blk == 0:
        return a
    pad = -(-seq // blk) * blk - seq
    return jnp.pad(a.reshape(nb, seq, -1), ((0, 0), (0, pad), (0, 0))).reshape(-1, a.shape[-1])


def _layer(x, k_cache, v_cache, c0, n0, m0, conv_buf, w, blk):
    nb, seq, d = x.shape
    x2d = x.reshape(nb * seq, d)
    qa, ka, va, qb, vb, ob, kbt, gates = _inproj(x2d, w["g_in"], w["w_main"], w["w_kbt"], w["w_gate"])

    if k_cache is None:
        att = _attn_prompt(qa, ka, va, w["gext"], w["g_att"], nb)
        keep = min(ATT_WINDOW, seq)
    else:
        att = _attn_step(qa, ka, va, k_cache, v_cache, w["gext"], w["g_att"])
        keep = seq
    heads = lambda a: a.reshape(nb, seq, N_HEADS_A, HEAD_DIM_A)[:, seq - keep:].transpose(0, 2, 1, 3).astype(F32)
    new_k, new_v = heads(ka), heads(va)

    nblk = -(-seq // blk)
    gi, gf = _gate_rows(gates, nb, seq, blk)
    bi_rows = jnp.tile(w["b_i"], nb * nblk)[:, None]
    bf_rows = jnp.tile(w["b_f"], nb * nblk)[:, None]
    ig, bcs = _gates(gi, gf, bi_rows, bf_rows, seq - (nblk - 1) * blk)
    kbt_p = kbt if seq % blk == 0 else jnp.pad(
        kbt.reshape(WIDTH_B, nb, seq), ((0, 0), (0, 0), (0, nblk * blk - seq))).reshape(WIDTH_B, -1)
    dh = HEAD_DIM_B
    c_ext0 = jnp.concatenate([c0, n0[..., None], jnp.zeros(c0.shape[:3] + (dh - 1,), F32)], axis=-1)
    m0b = jnp.broadcast_to(m0[..., None], m0.shape + (LANES,))
    hb, c_ext, m_b = _mlstm(_pad_tokens(qb, nb, seq, blk), kbt_p, _pad_tokens(vb, nb, seq, blk),
                            _pad_tokens(ob, nb, seq, blk), ig, bcs, w["g_mlstm"], c_ext0, m0b, blk)
    if seq % blk:
        hb = hb.reshape(nb, nblk * blk, WIDTH_B)[:, :seq].reshape(nb * seq, WIDTH_B)
    c1, n1, m1 = c_ext[..., :dh], c_ext[..., dh], m_b[..., 0]

    tm = min(FFN_ROWS, nb * seq)
    if seq < tm:
        init = jnp.pad(conv_buf, ((0, 0), (0, seq - (CONV_W - 1)), (0, 0))).reshape(nb * seq, -1)
        y, u = _ffn(x2d, att, hb, w["norm_g"], w["w_out"], w["w_up"], w["w_conv"], w["b_conv"],
                    w["w_down"], init, seq)
        new_buf = u.reshape(nb, seq, -1)[:, seq - (CONV_W - 1):]
    else:
        y, new_buf = _ffn(x2d, att, hb, w["norm_g"], w["w_out"], w["w_up"], w["w_conv"], w["b_conv"],
                          w["w_down"], conv_buf, seq)
    return y.reshape(nb, seq, d), new_k, new_v, c1, n1, m1, new_buf


def _layer_weights(l, norm_g, w_in, b_i, b_f, rel_table, g_att, g_mlstm, w_out, w_up, w_conv, b_conv, w_down):
    wa, wb = WIDTH_A, WIDTH_B
    wi = w_in[l]
    kb0 = 3 * wa + wb
    gate0 = 3 * wa + 4 * wb
    w_main = jnp.concatenate([wi[:, :kb0], wi[:, kb0 + wb:gate0]], axis=1).astype(BF16)
    return dict(
        g_in=norm_g[l, 0:1], norm_g=norm_g[l], w_main=w_main,
        w_kbt=wi[:, kb0:kb0 + wb].T.astype(BF16), w_gate=wi[:, gate0:].T,
        b_i=b_i[l], b_f=b_f[l], gext=_bias_rows(rel_table[l]), g_att=g_att[l][None, :],
        g_mlstm=g_mlstm[l][None, :], w_out=w_out[l].astype(BF16), w_up=w_up[l].astype(BF16),
        w_conv=w_conv[l], b_conv=b_conv[l][None, :], w_down=w_down[l].astype(BF16))


def kernel(x_prompt, x_sample, cache_k_att, cache_v_att, state_mlstm_c, state_mlstm_n, state_mlstm_m,
           cache_ffn_conv, norm_g, w_in, b_i, b_f, rel_table, g_att, g_mlstm, w_out, w_up, w_conv,
           b_conv, w_down):
    depth = w_in.shape[0]
    bp, sp, _ = x_prompt.shape
    d_ff2 = w_up.shape[2]
    dh = HEAD_DIM_B
    c_zero = jnp.zeros((bp, N_HEADS_B, dh, dh), F32)
    n_zero = jnp.zeros((bp, N_HEADS_B, dh), F32)
    m_zero = jnp.zeros((bp, N_HEADS_B), F32)
    buf_zero = jnp.zeros((bp, CONV_W - 1, d_ff2), F32)
    xp, xs = x_prompt, x_sample
    outs_p, outs_s = [], []
    for l in range(depth):
        w = _layer_weights(l, norm_g, w_in, b_i, b_f, rel_table, g_att, g_mlstm, w_out, w_up, w_conv,
                           b_conv, w_down)
        xp, *st_p = _layer(xp, None, None, c_zero, n_zero, m_zero, buf_zero, w, MLSTM_BLOCK)
        xs, *st_s = _layer(xs, cache_k_att[l], cache_v_att[l], state_mlstm_c[l], state_mlstm_n[l],
                           state_mlstm_m[l], cache_ffn_conv[l], w, MLSTM_BLOCK)
        outs_p.append(st_p)
        outs_s.append(st_s)
    stack = lambda outs, i: jnp.stack([o[i] for o in outs])
    return (xp, xs) + tuple(stack(outs_p, i) for i in range(6)) + tuple(stack(outs_s, i) for i in range(6))
```

```python
import functools
import math

import jax
import jax.numpy as jnp
from jax import lax
from jax.experimental import pallas as pl
from jax.experimental.pallas import tpu as pltpu

F32 = jnp.float32
BF16 = jnp.bfloat16

CHUNK = 64
LEFT_CHUNKS = 8
ATT_WINDOW = LEFT_CHUNKS * CHUNK
REL_CLIP = 256
N_HEADS_A = 8
HEAD_DIM_A = 64
WIDTH_A = N_HEADS_A * HEAD_DIM_A
N_HEADS_B = 4
HEAD_DIM_B = 128
WIDTH_B = N_HEADS_B * HEAD_DIM_B
CONV_W = 3
EPS = 1e-6
NEG = -1e30
LOG2E = math.log2(math.e)

LANES = 128
SUBLANES = 8
VMEM_LIMIT_BYTES = 56 * 1024 * 1024

INPROJ_ROWS = 512
ATT_ROWS = ATT_WINDOW
ATT_GROUP = 4 * CHUNK
ATT_BAND = ATT_GROUP + ATT_WINDOW
BIAS_W = 1024
PAIR_W = 2 * HEAD_DIM_A
MLSTM_BLOCK = 256
MLSTM_BATCHES = 2
FFN_ROWS = 256
FFN_COLS = 256


def _params(n_axes):
    return pltpu.CompilerParams(dimension_semantics=("arbitrary",) * n_axes,
                                vmem_limit_bytes=VMEM_LIMIT_BYTES)


def _const_spec(shape):
    nd = len(shape)
    return pl.BlockSpec(shape, lambda *_: (0,) * nd, pipeline_mode=pl.Buffered(1))


def _rms(x):
    return x * lax.rsqrt(jnp.mean(x * x, axis=-1, keepdims=True) + EPS)


def _dot(a, b):
    return jnp.dot(a, b, preferred_element_type=F32)


def _dot_nt(a, b):
    return lax.dot_general(a, b, (((1,), (1,)), ((), ())), preferred_element_type=F32)


def _inproj_body(x_ref, g_ref, wm_ref, wkt_ref, wg_ref, gb_ref,
                 qa_ref, va_ref, qb_ref, vb_ref, ob_ref, kt_ref, gcb_ref, *, seg):
    tm = x_ref.shape[0]
    h = _rms(x_ref[...]) * g_ref[...]
    hb = h.astype(BF16)
    outs = (qa_ref, va_ref, qb_ref, vb_ref, ob_ref)
    for c, o_ref in enumerate(outs):
        z = _dot(hb, wm_ref[:, c * WIDTH_A:(c + 1) * WIDTH_A])
        if c == 0:
            z = z * (HEAD_DIM_A ** -0.5 * LOG2E)
        o_ref[...] = z.astype(BF16)
    zk = _dot_nt(wkt_ref[...], hb)
    kt_ref[:WIDTH_A, :] = zk[:WIDTH_A].astype(BF16)
    kt_ref[WIDTH_A:, :] = (zk[WIDTH_A:] * (HEAD_DIM_B ** -0.5)).astype(BF16)
    lane = lax.broadcasted_iota(jnp.int32, (tm, LANES), 1)
    acc = jnp.zeros((tm, LANES), F32)
    for g in range(2 * N_HEADS_B):
        col = jnp.sum(h * wg_ref[g:g + 1, :], axis=-1, keepdims=True)
        acc = jnp.where(lane == g, col, acc)
    pre = acc.T[:2 * N_HEADS_B, :] + gb_ref[...]
    lf = jnp.minimum(pre, 0.0) - jnp.log(1.0 + jnp.exp(-jnp.abs(pre)))
    pos = lax.broadcasted_iota(jnp.int32, pre.shape, 1) & (seg - 1)
    k = 1
    while k < seg:
        lf = lf + jnp.where(pos >= k, pltpu.roll(lf, k, 1), 0.0)
        k *= 2
    row = lax.broadcasted_iota(jnp.int32, pre.shape, 0)
    gcb_ref[...] = jnp.where(row < N_HEADS_B, pre, lf)


def _inproj(x2d, g, wm, wkt, wg, gb, seg):
    t, d = x2d.shape
    tm = min(INPROJ_ROWS, t)
    assert t % tm == 0 and tm % seg == 0 and seg & (seg - 1) == 0
    row_spec = lambda w: pl.BlockSpec((tm, w), lambda i: (i, 0))
    col_spec = lambda r: pl.BlockSpec((r, tm), lambda i: (0, i))
    out_shape = [jax.ShapeDtypeStruct((t, WIDTH_A), BF16)] * 5 + [
        jax.ShapeDtypeStruct((WIDTH_A + WIDTH_B, t), BF16), jax.ShapeDtypeStruct((2 * N_HEADS_B, t), F32)]
    out_specs = [row_spec(WIDTH_A)] * 5 + [col_spec(WIDTH_A + WIDTH_B), col_spec(2 * N_HEADS_B)]
    return pl.pallas_call(
        functools.partial(_inproj_body, seg=seg), grid=(t // tm,),
        in_specs=[row_spec(d), _const_spec(g.shape), _const_spec(wm.shape), _const_spec(wkt.shape),
                  _const_spec(wg.shape), _const_spec(gb.shape)],
        out_specs=out_specs, out_shape=out_shape, compiler_params=_params(1), name="inproj",
    )(x2d, g, wm, wkt, wg, gb)


def _mlstm_body(*refs, n_valid, nbs):
    q_ref, v_ref, o_ref = refs[:3]
    kt_refs = refs[3:3 + nbs]
    gcb_refs = refs[3 + nbs:3 + 2 * nbs]
    g_ref, c0_ref, m0_ref, h_ref, c_out_ref, m_out_ref, c_sc, m_sc, vext_sc = refs[3 + 2 * nbs:]
    j = pl.program_id(1)
    blk = q_ref.shape[1]
    dh = HEAD_DIM_B

    @pl.when(j == 0)
    def _load_state():
        c_sc[...] = c0_ref[...]
        m_sc[...] = m0_ref[...]
        lane = lax.broadcasted_iota(jnp.int32, (blk, dh), 1)
        ones_col = jnp.where(lane == 0, 1.0, 0.0).astype(BF16)
        for s in range(nbs * N_HEADS_B):
            vext_sc[s, :, dh:] = ones_col

    row = lax.broadcasted_iota(jnp.int32, (blk, blk), 0)
    col = lax.broadcasted_iota(jnp.int32, (blk, blk), 1)
    lane1 = lax.broadcasted_iota(jnp.int32, (1, blk), 1)
    for bi in range(nbs):
        for hd in range(N_HEADS_B):
            s_idx = bi * N_HEADS_B + hd
            sl = slice(hd * dh, (hd + 1) * dh)
            q = q_ref[bi, :, sl]
            kt = kt_refs[bi][sl, :]
            vext_sc[s_idx, :, :dh] = v_ref[bi, :, sl]
            vext = vext_sc[s_idx]
            ig_r = gcb_refs[bi][hd:hd + 1, :]
            b_r = gcb_refs[bi][N_HEADS_B + hd:N_HEADS_B + hd + 1, :]
            if n_valid < blk:
                ig_r = jnp.where(lane1 < n_valid, ig_r, NEG)
                b_r = jnp.where(lane1 < n_valid, b_r, b_r[:, n_valid - 1:n_valid])
            m_prev = m_sc[s_idx:s_idx + 1, 0:1]
            b_c = jnp.sum(jnp.where(row == col, b_r, 0.0), axis=1, keepdims=True)
            dmat = jnp.where(col <= row, b_c - b_r + ig_r, NEG)
            inter = b_c + m_prev
            m_t = jnp.maximum(inter, jnp.max(dmat, axis=1, keepdims=True))
            w_intra = jnp.exp(dmat - m_t)
            w_inter = jnp.exp(inter - m_t)
            c_old = c_sc[bi, hd]
            qkc = _dot(q, jnp.concatenate([kt, c_old.astype(BF16)], axis=1))
            s = (qkc[:, :blk] * w_intra).astype(BF16)
            m_new = m_t[blk - 1:blk, :]
            b_last = b_r[:, blk - 1:blk]
            w_s = jnp.exp(b_last - b_r + ig_r - m_new)
            decay = jnp.exp(b_last + m_prev - m_new)
            ktw = (kt.astype(F32) * w_s).astype(BF16)
            sv = _dot(jnp.concatenate([s, ktw], axis=0), vext)
            ext = sv[:blk] + w_inter * qkc[:, blk:]
            hh = ext[:, :dh] / jnp.maximum(jnp.abs(ext[:, dh:dh + 1]), jnp.exp(-m_t))
            hh = _rms(hh)
            gate = 1.0 / (1.0 + jnp.exp(-o_ref[bi, :, sl].astype(F32)))
            h_ref[bi, :, sl] = (hh * g_ref[:, sl] * gate).astype(BF16)
            c_sc[bi, hd] = decay * c_old + sv[blk:]
            m_sc[s_idx:s_idx + 1, :] = jnp.broadcast_to(m_new, (1, m_sc.shape[1]))

    @pl.when(j == pl.num_programs(1) - 1)
    def _store_state():
        c_out_ref[...] = c_sc[...]
        m_out_ref[...] = m_sc[...]


def _mlstm(q, kt, v, o, gcb, g, c0, m0, blk, n_valid):
    nb, seq, _ = q.shape
    nblk = seq // blk
    nbs = math.gcd(nb, MLSTM_BATCHES)
    dh = HEAD_DIM_B
    tok = pl.BlockSpec((nbs, blk, WIDTH_B), lambda bg, j: (bg, j, 0))
    kt_specs = [pl.BlockSpec((WIDTH_B, blk), lambda bg, j, k=k: (WIDTH_A // WIDTH_B, (bg * nbs + k) * nblk + j))
                for k in range(nbs)]
    gcb_specs = [pl.BlockSpec((2 * N_HEADS_B, blk), lambda bg, j, k=k: (0, (bg * nbs + k) * nblk + j))
                 for k in range(nbs)]
    cspec = pl.BlockSpec((nbs, N_HEADS_B, dh, 2 * dh), lambda bg, j: (bg, 0, 0, 0))
    mspec = pl.BlockSpec((nbs * N_HEADS_B, LANES), lambda bg, j: (bg, 0))
    return pl.pallas_call(
        functools.partial(_mlstm_body, n_valid=n_valid, nbs=nbs), grid=(nb // nbs, nblk),
        in_specs=[tok, tok, tok] + kt_specs + gcb_specs + [pl.BlockSpec(g.shape, lambda bg, j: (0, 0)),
                                                          cspec, mspec],
        out_specs=[tok, cspec, mspec],
        out_shape=[jax.ShapeDtypeStruct(q.shape, BF16), jax.ShapeDtypeStruct(c0.shape, F32),
                   jax.ShapeDtypeStruct(m0.shape, F32)],
        scratch_shapes=[pltpu.VMEM((nbs, N_HEADS_B, dh, 2 * dh), F32), pltpu.VMEM((nbs * N_HEADS_B, LANES), F32),
                        pltpu.VMEM((nbs * N_HEADS_B, blk, 2 * dh), BF16)],
        compiler_params=_params(2), name="mlstm",
    )(q, v, o, *([kt] * nbs), *([gcb] * nbs), g, c0, m0)


def _toeplitz_bias(gext_row, rows):
    gx = jnp.broadcast_to(gext_row, (rows, BIAS_W))
    return pltpu.roll(gx, 0, 1, stride=1, stride_axis=0)


def _attn_prompt_body(q_ref, kp_ref, kc_ref, vp_ref, vc_ref, gext_ref, gatt_ref, o_ref, bias_sc, o_sc):
    b = pl.program_id(0)
    i = pl.program_id(1)
    n_pairs = N_HEADS_A // 2

    @pl.when((b == 0) & (i == 0))
    def _init_bias():
        r = lax.broadcasted_iota(jnp.int32, (ATT_GROUP, BIAS_W), 0)
        c = lax.broadcasted_iota(jnp.int32, (ATT_GROUP, BIAS_W), 1)
        first = r - (r & (CHUNK - 1))
        ok = (c >= first) & (c < first + ATT_WINDOW + CHUNK)
        for hd in range(N_HEADS_A):
            t = _toeplitz_bias(gext_ref[hd:hd + 1, :], ATT_GROUP) * LOG2E
            bias_sc[hd // 2, (hd % 2) * ATT_GROUP:(hd % 2 + 1) * ATT_GROUP, :] = jnp.where(ok, t, NEG)[:, :ATT_BAND]

    lane = lax.broadcasted_iota(jnp.int32, (ATT_GROUP, PAIR_W), 1)
    low = lane < HEAD_DIM_A

    def step(has_prev):
        for g in range(ATT_ROWS // ATT_GROUP):
            r0 = g * ATT_GROUP
            n_prev = ATT_WINDOW - r0 if has_prev else 0
            n_cur = r0 + ATT_GROUP
            for p in range(n_pairs):
                ps = slice(p * PAIR_W, (p + 1) * PAIR_W)
                qp = q_ref[r0:r0 + ATT_GROUP, ps]
                zero = jnp.zeros_like(qp)
                qq = jnp.concatenate([jnp.where(low, qp, zero), jnp.where(low, zero, qp)], axis=0)
                s_cur = _dot(qq, kc_ref[ps, :n_cur])
                bias = bias_sc[p]
                if has_prev:
                    s = jnp.concatenate([_dot(qq, kp_ref[ps, r0:]), s_cur], axis=1) + bias
                else:
                    s = s_cur + bias[:, ATT_BAND - n_cur:]
                e = jnp.exp2(s - jnp.max(s, axis=-1, keepdims=True))
                l = jnp.sum(e, axis=-1, keepdims=True)
                eb = e.astype(BF16)
                o = _dot(eb[:, n_prev:], vc_ref[:n_cur, ps])
                if has_prev:
                    o = o + _dot(eb[:, :n_prev], vp_ref[r0:, ps])
                o = o / l
                o_sc[:, ps] = jnp.where(low, o[:ATT_GROUP], o[ATT_GROUP:])
            o_ref[r0:r0 + ATT_GROUP, :] = (_rms(o_sc[...]) * gatt_ref[...]).astype(BF16)

    @pl.when(i == 0)
    def _first_block():
        step(False)

    @pl.when(i > 0)
    def _later_block():
        step(True)


def _attn_prompt(qa, kt, va, gext, gatt, nb):
    t = qa.shape[0]
    nblk = t // (nb * ATT_ROWS)
    cur = pl.BlockSpec((ATT_ROWS, WIDTH_A), lambda b, i: (b * nblk + i, 0))
    prev = pl.BlockSpec((ATT_ROWS, WIDTH_A), lambda b, i: (b * nblk + jnp.maximum(i - 1, 0), 0))
    cur_t = pl.BlockSpec((WIDTH_A, ATT_ROWS), lambda b, i: (0, b * nblk + i))
    prev_t = pl.BlockSpec((WIDTH_A, ATT_ROWS), lambda b, i: (0, b * nblk + jnp.maximum(i - 1, 0)))
    return pl.pallas_call(
        _attn_prompt_body, grid=(nb, nblk),
        in_specs=[cur, prev_t, cur_t, prev, cur, pl.BlockSpec(gext.shape, lambda b, i: (0, 0)),
                  pl.BlockSpec(gatt.shape, lambda b, i: (0, 0))],
        out_specs=cur, out_shape=jax.ShapeDtypeStruct((t, WIDTH_A), BF16),
        scratch_shapes=[pltpu.VMEM((N_HEADS_A // 2, 2 * ATT_GROUP, ATT_BAND), F32),
                        pltpu.VMEM((ATT_GROUP, WIDTH_A), F32)],
        compiler_params=_params(2), name="attn_prompt",
    )(qa, kt, kt, va, va, gext, gatt)


def _attn_step_body(q_ref, kn_ref, vn_ref, ck_ref, cv_ref, gext_ref, gatt_ref, o_ref, o_sc):
    s_new = q_ref.shape[0]
    l_cache = ck_ref.shape[1]
    for hd in range(N_HEADS_A):
        sl = slice(hd * HEAD_DIM_A, (hd + 1) * HEAD_DIM_A)
        bias = _toeplitz_bias(gext_ref[hd:hd + 1, :], s_new) * LOG2E
        q = q_ref[:, sl]
        s1 = _dot_nt(q, ck_ref[hd].astype(BF16)) + bias[:, :l_cache]
        s2 = _dot(q, kn_ref[sl, :]) + bias[:, l_cache:l_cache + s_new]
        m = jnp.maximum(jnp.max(s1, axis=-1, keepdims=True), jnp.max(s2, axis=-1, keepdims=True))
        p1 = jnp.exp2(s1 - m)
        p2 = jnp.exp2(s2 - m)
        l = jnp.sum(p1, axis=-1, keepdims=True) + jnp.sum(p2, axis=-1, keepdims=True)
        o = _dot(p1.astype(BF16), cv_ref[hd].astype(BF16)) + _dot(p2.astype(BF16), vn_ref[:, sl])
        o_sc[:, sl] = o / l
    o_ref[...] = (_rms(o_sc[...]) * gatt_ref[...]).astype(BF16)


def _attn_step(qa, kt_new, va, ck, cv, gext, gatt):
    nb, nh, l_cache, dh = ck.shape
    s_new = qa.shape[0] // nb
    tok = pl.BlockSpec((s_new, WIDTH_A), lambda b: (b, 0))
    cache = pl.BlockSpec((None, nh, l_cache, dh), lambda b: (b, 0, 0, 0))
    return pl.pallas_call(
        _attn_step_body, grid=(nb,),
        in_specs=[tok, pl.BlockSpec((None, WIDTH_A, s_new), lambda b: (b, 0, 0)), tok, cache, cache,
                  pl.BlockSpec(gext.shape, lambda b: (0, 0)), pl.BlockSpec(gatt.shape, lambda b: (0, 0))],
        out_specs=tok, out_shape=jax.ShapeDtypeStruct(qa.shape, BF16),
        scratch_shapes=[pltpu.VMEM((s_new, WIDTH_A), F32)],
        compiler_params=_params(1), name="attn_step",
    )(qa, kt_new, va, ck, cv, gext, gatt)


def _gelu_tanh(x):
    return 0.5 * x * (1.0 + jnp.tanh(0.7978845608028654 * (x + 0.044715 * (x * x * x))))


def _ffn_body(x_ref, att_ref, ml_ref, ng_ref, wo_ref, wu_ref, wc_ref, bc_ref, wd_ref, init_ref,
              y_ref, tail_ref, carry_sc, a_sc, *, seq_rows):
    tm = x_ref.shape[0]
    d_ff = wd_ref.shape[0]
    multi_seq = seq_rows < tm
    mix = _dot(att_ref[...], wo_ref[:WIDTH_A, :]) + _dot(ml_ref[...], wo_ref[WIDTH_A:, :])
    x1 = x_ref[...] + _rms(mix) * ng_ref[1:2, :]
    h2 = (_rms(x1) * ng_ref[2:3, :]).astype(BF16)

    row = lax.broadcasted_iota(jnp.int32, (tm, FFN_COLS), 0)
    if multi_seq:
        pos = row % seq_rows
    else:
        pos = row

        @pl.when(pl.program_id(0) % (seq_rows // tm) == 0)
        def _seq_start():
            carry_sc[...] = init_ref[...]

    def conv(cols):
        u = _dot(h2, wu_ref[:, cols])
        if multi_seq:
            fix = init_ref[:, cols]
            um2 = jnp.where(pos < 2, fix, pltpu.roll(u, 2, 0))
            um1 = jnp.where(pos < 1, pltpu.roll(fix, tm - 1, 0), pltpu.roll(u, 1, 0))
            tail_ref[:, cols] = u
        else:
            c0 = carry_sc[0:1, cols]
            c1 = carry_sc[1:2, cols]
            um2 = jnp.where(pos >= 2, pltpu.roll(u, 2, 0), jnp.where(pos == 0, c0, c1))
            um1 = jnp.where(pos >= 1, pltpu.roll(u, 1, 0), c1)
            carry_sc[:, cols] = u[tm - 2:, :]
            tail_ref[:, cols] = u[tm - 2:, :]
        return bc_ref[:, cols] + um2 * wc_ref[0:1, cols] + um1 * wc_ref[1:2, cols] + u * wc_ref[2:3, cols]

    for c in range(d_ff // FFN_COLS):
        gate = conv(slice(c * FFN_COLS, (c + 1) * FFN_COLS))
        up = conv(slice(d_ff + c * FFN_COLS, d_ff + (c + 1) * FFN_COLS))
        a_sc[:, c * FFN_COLS:(c + 1) * FFN_COLS] = (_gelu_tanh(gate) * up).astype(BF16)
    f = _dot(a_sc[...], wd_ref[...])
    y_ref[...] = x1 + _rms(f) * ng_ref[3:4, :]


def _ffn(x2d, att, ml, ng, wo, wu, wc, bc, wd, init, seq_rows):
    t, d = x2d.shape
    d_ff = wd.shape[0]
    tm = min(FFN_ROWS, t)
    assert t % tm == 0 and d_ff % FFN_COLS == 0
    multi_seq = seq_rows < tm
    if multi_seq:
        assert tm % seq_rows == 0 and t == tm
        init_spec = pl.BlockSpec(init.shape, lambda i: (0, 0))
        tail_spec = pl.BlockSpec((tm, 2 * d_ff), lambda i: (0, 0))
        tail_shape = jax.ShapeDtypeStruct((tm, 2 * d_ff), F32)
    else:
        assert seq_rows % tm == 0
        per_seq = seq_rows // tm
        init_spec = pl.BlockSpec((None, CONV_W - 1, 2 * d_ff), lambda i: (i // per_seq, 0, 0))
        tail_spec = init_spec
        tail_shape = jax.ShapeDtypeStruct(init.shape, F32)
    row_spec = lambda w: pl.BlockSpec((tm, w), lambda i: (i, 0))
    return pl.pallas_call(
        functools.partial(_ffn_body, seq_rows=seq_rows), grid=(t // tm,),
        in_specs=[row_spec(d), row_spec(WIDTH_A), row_spec(WIDTH_B), _const_spec(ng.shape),
                  _const_spec(wo.shape), _const_spec(wu.shape), _const_spec(wc.shape),
                  _const_spec(bc.shape), _const_spec(wd.shape), init_spec],
        out_specs=[row_spec(d), tail_spec],
        out_shape=[jax.ShapeDtypeStruct((t, d), F32), tail_shape],
        scratch_shapes=[pltpu.VMEM((CONV_W - 1, 2 * d_ff), F32), pltpu.VMEM((tm, d_ff), BF16)],
        compiler_params=_params(1), name="ffn",
    )(x2d, att, ml, ng, wo, wu, wc, bc, wd, init)


def _bias_rows(rel_table_l):
    n_far = ATT_WINDOW - REL_CLIP + 1
    far = jnp.broadcast_to(rel_table_l[:, 2 * REL_CLIP:], (N_HEADS_A, n_far))
    lo = ATT_WINDOW + REL_CLIP - (ATT_BAND - 1)
    near = rel_table_l[:, lo:2 * REL_CLIP][:, ::-1]
    wrap = jnp.broadcast_to(rel_table_l[:, 2 * REL_CLIP:], (N_HEADS_A, BIAS_W - ATT_BAND))
    return jnp.concatenate([far, near, wrap], axis=1)


def _pad_seq(a, seq_pad):
    return a if a.shape[1] == seq_pad else jnp.pad(a, ((0, 0), (0, seq_pad - a.shape[1]), (0, 0)))


def _pad_cols(a, nb, seq, seq_pad):
    if seq == seq_pad:
        return a
    return jnp.pad(a.reshape(-1, nb, seq), ((0, 0), (0, 0), (0, seq_pad - seq))).reshape(a.shape[0], -1)


def _layer(x, k_cache, v_cache, c0, n0, m0, conv_buf, w):
    nb, seq, d = x.shape
    x2d = x.reshape(nb * seq, d)
    blk = MLSTM_BLOCK if seq % MLSTM_BLOCK == 0 else LANES
    seg = min(blk, seq)
    qa, va, qb, vb, ob, kt, gcb = _inproj(x2d, w["g_in"], w["w_main"], w["w_kt"], w["w_gate"], w["gate_bias"], seg)

    kat = kt[:WIDTH_A].reshape(N_HEADS_A, HEAD_DIM_A, nb, seq)
    if k_cache is None:
        att = _attn_prompt(qa, kt, va, w["gext"], w["g_att"], nb)
        keep = min(ATT_WINDOW, seq)
    else:
        att = _attn_step(qa, kt[:WIDTH_A].reshape(WIDTH_A, nb, seq).transpose(1, 0, 2), va, k_cache, v_cache,
                         w["gext"], w["g_att"])
        keep = seq
    new_k = kat[..., seq - keep:].transpose(2, 0, 3, 1).astype(F32)
    new_v = va.reshape(nb, seq, N_HEADS_A, HEAD_DIM_A)[:, seq - keep:].transpose(0, 2, 1, 3).astype(F32)

    seq_pad = -(-seq // blk) * blk
    dh = HEAD_DIM_B
    c_ext0 = jnp.concatenate([c0, n0[..., None], jnp.zeros(c0.shape[:3] + (dh - 1,), F32)], axis=-1)
    m0b = jnp.broadcast_to(m0.reshape(-1, 1), (nb * N_HEADS_B, LANES))
    r3 = lambda a: _pad_seq(a.reshape(nb, seq, -1), seq_pad)
    hb, c_ext, m_b = _mlstm(r3(qb), _pad_cols(kt, nb, seq, seq_pad), r3(vb), r3(ob),
                            _pad_cols(gcb, nb, seq, seq_pad), w["g_mlstm"], c_ext0, m0b, blk,
                            seq - (seq_pad - blk))
    hb = hb[:, :seq].reshape(nb * seq, WIDTH_B)
    c1, n1, m1 = c_ext[..., :dh], c_ext[..., dh], m_b[:, 0].reshape(nb, N_HEADS_B)

    tm = min(FFN_ROWS, nb * seq)
    if seq < tm:
        init = jnp.pad(conv_buf, ((0, 0), (0, seq - (CONV_W - 1)), (0, 0))).reshape(nb * seq, -1)
        y, u = _ffn(x2d, att, hb, w["norm_g"], w["w_out"], w["w_up"], w["w_conv"], w["b_conv"],
                    w["w_down"], init, seq)
        new_buf = u.reshape(nb, seq, -1)[:, seq - (CONV_W - 1):]
    else:
        y, new_buf = _ffn(x2d, att, hb, w["norm_g"], w["w_out"], w["w_up"], w["w_conv"], w["b_conv"],
                          w["w_down"], conv_buf, seq)
    return y.reshape(nb, seq, d), new_k, new_v, c1, n1, m1, new_buf


def _layer_weights(l, norm_g, w_in, b_i, b_f, rel_table, g_att, g_mlstm, w_out, w_up, w_conv, b_conv, w_down):
    wa, wb = WIDTH_A, WIDTH_B
    wi = w_in[l]
    col = lambda k: slice(k * wa, (k + 1) * wa)
    assert wa == wb
    w_main = jnp.concatenate([wi[:, col(0)], wi[:, col(2)], wi[:, col(3)], wi[:, col(5)], wi[:, col(6)]],
                             axis=1).astype(BF16)
    w_kt = jnp.concatenate([wi[:, col(1)], wi[:, col(4)]], axis=1).T.astype(BF16)
    return dict(
        g_in=norm_g[l, 0:1], norm_g=norm_g[l], w_main=w_main, w_kt=w_kt, w_gate=wi[:, 7 * wa:].T,
        gate_bias=jnp.concatenate([b_i[l], b_f[l]])[:, None], gext=_bias_rows(rel_table[l]),
        g_att=g_att[l][None, :], g_mlstm=g_mlstm[l][None, :], w_out=w_out[l].astype(BF16),
        w_up=w_up[l].astype(BF16), w_conv=w_conv[l], b_conv=b_conv[l][None, :], w_down=w_down[l].astype(BF16))


def kernel(x_prompt, x_sample, cache_k_att, cache_v_att, state_mlstm_c, state_mlstm_n, state_mlstm_m,
           cache_ffn_conv, norm_g, w_in, b_i, b_f, rel_table, g_att, g_mlstm, w_out, w_up, w_conv,
           b_conv, w_down):
    depth = w_in.shape[0]
    bp = x_prompt.shape[0]
    d_ff2 = w_up.shape[2]
    dh = HEAD_DIM_B
    c_zero = jnp.zeros((bp, N_HEADS_B, dh, dh), F32)
    n_zero = jnp.zeros((bp, N_HEADS_B, dh), F32)
    m_zero = jnp.zeros((bp, N_HEADS_B), F32)
    buf_zero = jnp.zeros((bp, CONV_W - 1, d_ff2), F32)
    xp, xs = x_prompt, x_sample
    outs_p, outs_s = [], []
    for l in range(depth):
        w = _layer_weights(l, norm_g, w_in, b_i, b_f, rel_table, g_att, g_mlstm, w_out, w_up, w_conv,
                           b_conv, w_down)
        xp, *st_p = _layer(xp, None, None, c_zero, n_zero, m_zero, buf_zero, w)
        xs, *st_s = _layer(xs, cache_k_att[l], cache_v_att[l], state_mlstm_c[l], state_mlstm_n[l],
                           state_mlstm_m[l], cache_ffn_conv[l], w)
        outs_p.append(st_p)
        outs_s.append(st_s)
    stack = lambda outs, i: jnp.stack([o[i] for o in outs])
    return (xp, xs) + tuple(stack(outs_p, i) for i in range(6)) + tuple(stack(outs_s, i) for i in range(6))
```

```python
import functools
import math

import jax
import jax.numpy as jnp
from jax import lax
from jax.experimental import pallas as pl
from jax.experimental.pallas import tpu as pltpu

F32 = jnp.float32
BF16 = jnp.bfloat16

CHUNK = 64
LEFT_CHUNKS = 8
ATT_WINDOW = LEFT_CHUNKS * CHUNK
REL_CLIP = 256
N_HEADS_A = 8
HEAD_DIM_A = 64
WIDTH_A = N_HEADS_A * HEAD_DIM_A
N_HEADS_B = 4
HEAD_DIM_B = 128
WIDTH_B = N_HEADS_B * HEAD_DIM_B
CONV_W = 3
EPS = 1e-6
NEG = -1e30
LOG2E = math.log2(math.e)

LANES = 128
SUBLANES = 8
VMEM_LIMIT_BYTES = 56 * 1024 * 1024

INPROJ_ROWS = 512
ATT_ROWS = ATT_WINDOW
ATT_GROUP = 4 * CHUNK
ATT_BAND = ATT_GROUP + ATT_WINDOW
BIAS_W = 1024
PAIR_W = 2 * HEAD_DIM_A
MLSTM_BLOCK = 256
MLSTM_BATCHES = 2
FFN_ROWS = 256
FFN_COLS = 256


def _params(n_axes):
    return pltpu.CompilerParams(dimension_semantics=("arbitrary",) * n_axes,
                                vmem_limit_bytes=VMEM_LIMIT_BYTES)


def _const_spec(shape):
    nd = len(shape)
    return pl.BlockSpec(shape, lambda *_: (0,) * nd, pipeline_mode=pl.Buffered(1))


def _rms(x):
    return x * lax.rsqrt(jnp.mean(x * x, axis=-1, keepdims=True) + EPS)


def _dot(a, b):
    return jnp.dot(a, b, preferred_element_type=F32)


def _dot_nt(a, b):
    return lax.dot_general(a, b, (((1,), (1,)), ((), ())), preferred_element_type=F32)


def _inproj_body(x_ref, g_ref, wm_ref, wkt_ref, gb_ref,
                 qa_ref, va_ref, qb_ref, vb_ref, ob_ref, kt_ref, gcb_ref, *, seg):
    n_keys = WIDTH_A + WIDTH_B
    hb = (_rms(x_ref[...]) * g_ref[...]).astype(BF16)
    zk = _dot_nt(wkt_ref[...], hb)
    kt_ref[:WIDTH_A, :] = zk[:WIDTH_A].astype(BF16)
    kt_ref[WIDTH_A:, :] = (zk[WIDTH_A:n_keys] * (HEAD_DIM_B ** -0.5)).astype(BF16)
    pre = zk[n_keys:n_keys + 2 * N_HEADS_B] + zk[n_keys + 2 * N_HEADS_B:] + gb_ref[...]
    lf = jnp.minimum(pre, 0.0) - jnp.log(1.0 + jnp.exp(-jnp.abs(pre)))
    pos = lax.broadcasted_iota(jnp.int32, pre.shape, 1) & (seg - 1)
    k = 1
    while k < seg:
        lf = lf + jnp.where(pos >= k, pltpu.roll(lf, k, 1), 0.0)
        k *= 2
    row = lax.broadcasted_iota(jnp.int32, pre.shape, 0)
    gcb_ref[...] = jnp.where(row < N_HEADS_B, pre, lf)
    outs = (qa_ref, va_ref, qb_ref, vb_ref, ob_ref)
    for c, o_ref in enumerate(outs):
        z = _dot(hb, wm_ref[:, c * WIDTH_A:(c + 1) * WIDTH_A])
        if c == 0:
            z = z * (HEAD_DIM_A ** -0.5 * LOG2E)
        o_ref[...] = z.astype(BF16)


def _inproj(x2d, g, wm, wkt, gb, seg):
    t, d = x2d.shape
    tm = min(INPROJ_ROWS, t)
    assert t % tm == 0 and tm % seg == 0 and seg & (seg - 1) == 0
    row_spec = lambda w: pl.BlockSpec((tm, w), lambda i: (i, 0))
    col_spec = lambda r: pl.BlockSpec((r, tm), lambda i: (0, i))
    out_shape = [jax.ShapeDtypeStruct((t, WIDTH_A), BF16)] * 5 + [
        jax.ShapeDtypeStruct((WIDTH_A + WIDTH_B, t), BF16), jax.ShapeDtypeStruct((2 * N_HEADS_B, t), F32)]
    out_specs = [row_spec(WIDTH_A)] * 5 + [col_spec(WIDTH_A + WIDTH_B), col_spec(2 * N_HEADS_B)]
    return pl.pallas_call(
        functools.partial(_inproj_body, seg=seg), grid=(t // tm,),
        in_specs=[row_spec(d), _const_spec(g.shape), _const_spec(wm.shape), _const_spec(wkt.shape),
                  _const_spec(gb.shape)],
        out_specs=out_specs, out_shape=out_shape, compiler_params=_params(1), name="inproj",
    )(x2d, g, wm, wkt, gb)


def _mlstm_body(*refs, n_valid, nbs):
    q_ref, v_ref, o_ref = refs[:3]
    kt_refs = refs[3:3 + nbs]
    gcb_refs = refs[3 + nbs:3 + 2 * nbs]
    g_ref, c0_ref, m0_ref, h_ref, c_out_ref, m_out_ref, c_sc, m_sc, vext_sc = refs[3 + 2 * nbs:]
    j = pl.program_id(1)
    blk = q_ref.shape[1]
    dh = HEAD_DIM_B

    @pl.when(j == 0)
    def _load_state():
        c_sc[...] = c0_ref[...]
        m_sc[...] = m0_ref[...]
        lane = lax.broadcasted_iota(jnp.int32, (blk, dh), 1)
        ones_col = jnp.where(lane == 0, 1.0, 0.0).astype(BF16)
        for s in range(nbs * N_HEADS_B):
            vext_sc[s, :, dh:] = ones_col

    row = lax.broadcasted_iota(jnp.int32, (blk, blk), 0)
    col = lax.broadcasted_iota(jnp.int32, (blk, blk), 1)
    lane1 = lax.broadcasted_iota(jnp.int32, (1, blk), 1)
    for bi in range(nbs):
        for hd in range(N_HEADS_B):
            s_idx = bi * N_HEADS_B + hd
            sl = slice(hd * dh, (hd + 1) * dh)
            q = q_ref[bi, :, sl]
            kt = kt_refs[bi][sl, :]
            vext_sc[s_idx, :, :dh] = v_ref[bi, :, sl]
            vext = vext_sc[s_idx]
            ig_r = gcb_refs[bi][hd:hd + 1, :]
            b_r = gcb_refs[bi][N_HEADS_B + hd:N_HEADS_B + hd + 1, :]
            if n_valid < blk:
                ig_r = jnp.where(lane1 < n_valid, ig_r, NEG)
                b_r = jnp.where(lane1 < n_valid, b_r, b_r[:, n_valid - 1:n_valid])
            m_prev = m_sc[s_idx:s_idx + 1, 0:1]
            b_c = jnp.sum(jnp.where(row == col, b_r, 0.0), axis=1, keepdims=True)
            dmat = jnp.where(col <= row, b_c - b_r + ig_r, NEG)
            inter = b_c + m_prev
            m_t = jnp.maximum(inter, jnp.max(dmat, axis=1, keepdims=True))
            w_intra = jnp.exp(dmat - m_t)
            w_inter = jnp.exp(inter - m_t)
            c_old = c_sc[bi, hd]
            qkc = _dot(q, jnp.concatenate([kt, c_old.astype(BF16)], axis=1))
            s = (qkc[:, :blk] * w_intra).astype(BF16)
            m_new = m_t[blk - 1:blk, :]
            b_last = b_r[:, blk - 1:blk]
            w_s = jnp.exp(b_last - b_r + ig_r - m_new)
            decay = jnp.exp(b_last + m_prev - m_new)
            ktw = (kt.astype(F32) * w_s).astype(BF16)
            sv = _dot(jnp.concatenate([s, ktw], axis=0), vext)
            ext = sv[:blk] + w_inter * qkc[:, blk:]
            hh = ext[:, :dh] / jnp.maximum(jnp.abs(ext[:, dh:dh + 1]), jnp.exp(-m_t))
            hh = _rms(hh)
            gate = 1.0 / (1.0 + jnp.exp(-o_ref[bi, :, sl].astype(F32)))
            h_ref[bi, :, sl] = (hh * g_ref[:, sl] * gate).astype(BF16)
            c_sc[bi, hd] = decay * c_old + sv[blk:]
            m_sc[s_idx:s_idx + 1, :] = jnp.broadcast_to(m_new, (1, m_sc.shape[1]))

    @pl.when(j == pl.num_programs(1) - 1)
    def _store_state():
        c_out_ref[...] = c_sc[...]
        m_out_ref[...] = m_sc[...]


def _mlstm(q, kt, v, o, gcb, g, c0, m0, blk, n_valid):
    nb, seq, _ = q.shape
    nblk = seq // blk
    nbs = math.gcd(nb, MLSTM_BATCHES)
    dh = HEAD_DIM_B
    tok = pl.BlockSpec((nbs, blk, WIDTH_B), lambda bg, j: (bg, j, 0))
    kt_specs = [pl.BlockSpec((WIDTH_B, blk), lambda bg, j, k=k: (WIDTH_A // WIDTH_B, (bg * nbs + k) * nblk + j))
                for k in range(nbs)]
    gcb_specs = [pl.BlockSpec((2 * N_HEADS_B, blk), lambda bg, j, k=k: (0, (bg * nbs + k) * nblk + j))
                 for k in range(nbs)]
    cspec = pl.BlockSpec((nbs, N_HEADS_B, dh, 2 * dh), lambda bg, j: (bg, 0, 0, 0))
    mspec = pl.BlockSpec((nbs * N_HEADS_B, LANES), lambda bg, j: (bg, 0))
    return pl.pallas_call(
        functools.partial(_mlstm_body, n_valid=n_valid, nbs=nbs), grid=(nb // nbs, nblk),
        in_specs=[tok, tok, tok] + kt_specs + gcb_specs + [pl.BlockSpec(g.shape, lambda bg, j: (0, 0)),
                                                          cspec, mspec],
        out_specs=[tok, cspec, mspec],
        out_shape=[jax.ShapeDtypeStruct(q.shape, BF16), jax.ShapeDtypeStruct(c0.shape, F32),
                   jax.ShapeDtypeStruct(m0.shape, F32)],
        scratch_shapes=[pltpu.VMEM((nbs, N_HEADS_B, dh, 2 * dh), F32), pltpu.VMEM((nbs * N_HEADS_B, LANES), F32),
                        pltpu.VMEM((nbs * N_HEADS_B, blk, 2 * dh), BF16)],
        compiler_params=_params(2), name="mlstm",
    )(q, v, o, *([kt] * nbs), *([gcb] * nbs), g, c0, m0)


def _toeplitz_bias(gext_row, rows):
    gx = jnp.broadcast_to(gext_row, (rows, BIAS_W))
    return pltpu.roll(gx, 0, 1, stride=1, stride_axis=0)


def _attn_prompt_body(q_ref, kp_ref, kc_ref, vp_ref, vc_ref, gext_ref, gatt_ref, o_ref, bias_sc, o_sc):
    b = pl.program_id(0)
    i = pl.program_id(1)
    n_pairs = N_HEADS_A // 2

    @pl.when((b == 0) & (i == 0))
    def _init_bias():
        r = lax.broadcasted_iota(jnp.int32, (ATT_GROUP, BIAS_W), 0)
        c = lax.broadcasted_iota(jnp.int32, (ATT_GROUP, BIAS_W), 1)
        first = r - (r & (CHUNK - 1))
        ok = (c >= first) & (c < first + ATT_WINDOW + CHUNK)
        for hd in range(N_HEADS_A):
            t = _toeplitz_bias(gext_ref[hd:hd + 1, :], ATT_GROUP) * LOG2E
            bias_sc[hd // 2, (hd % 2) * ATT_GROUP:(hd % 2 + 1) * ATT_GROUP, :] = jnp.where(ok, t, NEG)[:, :ATT_BAND]

    lane = lax.broadcasted_iota(jnp.int32, (ATT_GROUP, PAIR_W), 1)
    low = lane < HEAD_DIM_A

    def step(has_prev):
        for g in range(ATT_ROWS // ATT_GROUP):
            r0 = g * ATT_GROUP
            n_prev = ATT_WINDOW - r0 if has_prev else 0
            n_cur = r0 + ATT_GROUP
            for p in range(n_pairs):
                ps = slice(p * PAIR_W, (p + 1) * PAIR_W)
                qp = q_ref[r0:r0 + ATT_GROUP, ps]
                zero = jnp.zeros_like(qp)
                qq = jnp.concatenate([jnp.where(low, qp, zero), jnp.where(low, zero, qp)], axis=0)
                s_cur = _dot(qq, kc_ref[ps, :n_cur])
                bias = bias_sc[p]
                if has_prev:
                    s = jnp.concatenate([_dot(qq, kp_ref[ps, r0:]), s_cur], axis=1) + bias
                else:
                    s = s_cur + bias[:, ATT_BAND - n_cur:]
                e = jnp.exp2(s - jnp.max(s, axis=-1, keepdims=True))
                l = jnp.sum(e, axis=-1, keepdims=True)
                eb = e.astype(BF16)
                o = _dot(eb[:, n_prev:], vc_ref[:n_cur, ps])
                if has_prev:
                    o = o + _dot(eb[:, :n_prev], vp_ref[r0:, ps])
                o = o / l
                o_sc[:, ps] = jnp.where(low, o[:ATT_GROUP], o[ATT_GROUP:])
            o_ref[r0:r0 + ATT_GROUP, :] = (_rms(o_sc[...]) * gatt_ref[...]).astype(BF16)

    @pl.when(i == 0)
    def _first_block():
        step(False)

    @pl.when(i > 0)
    def _later_block():
        step(True)


def _attn_prompt(qa, kt, va, gext, gatt, nb):
    t = qa.shape[0]
    nblk = t // (nb * ATT_ROWS)
    cur = pl.BlockSpec((ATT_ROWS, WIDTH_A), lambda b, i: (b * nblk + i, 0))
    prev = pl.BlockSpec((ATT_ROWS, WIDTH_A), lambda b, i: (b * nblk + jnp.maximum(i - 1, 0), 0))
    cur_t = pl.BlockSpec((WIDTH_A, ATT_ROWS), lambda b, i: (0, b * nblk + i))
    prev_t = pl.BlockSpec((WIDTH_A, ATT_ROWS), lambda b, i: (0, b * nblk + jnp.maximum(i - 1, 0)))
    return pl.pallas_call(
        _attn_prompt_body, grid=(nb, nblk),
        in_specs=[cur, prev_t, cur_t, prev, cur, pl.BlockSpec(gext.shape, lambda b, i: (0, 0)),
                  pl.BlockSpec(gatt.shape, lambda b, i: (0, 0))],
        out_specs=cur, out_shape=jax.ShapeDtypeStruct((t, WIDTH_A), BF16),
        scratch_shapes=[pltpu.VMEM((N_HEADS_A // 2, 2 * ATT_GROUP, ATT_BAND), F32),
                        pltpu.VMEM((ATT_GROUP, WIDTH_A), F32)],
        compiler_params=_params(2), name="attn_prompt",
    )(qa, kt, kt, va, va, gext, gatt)


def _attn_step_body(q_ref, kn_ref, vn_ref, ck_ref, cv_ref, gext_ref, gatt_ref, o_ref, o_sc):
    s_new = q_ref.shape[0]
    l_cache = ck_ref.shape[2]
    for hd in range(N_HEADS_A):
        sl = slice(hd * HEAD_DIM_A, (hd + 1) * HEAD_DIM_A)
        bias = _toeplitz_bias(gext_ref[hd:hd + 1, :], s_new) * LOG2E
        q = q_ref[:, sl]
        s1 = _dot(q, ck_ref[hd].astype(BF16)) + bias[:, :l_cache]
        s2 = _dot(q, kn_ref[sl, :]) + bias[:, l_cache:l_cache + s_new]
        m = jnp.maximum(jnp.max(s1, axis=-1, keepdims=True), jnp.max(s2, axis=-1, keepdims=True))
        p1 = jnp.exp2(s1 - m)
        p2 = jnp.exp2(s2 - m)
        l = jnp.sum(p1, axis=-1, keepdims=True) + jnp.sum(p2, axis=-1, keepdims=True)
        o = _dot_nt(p1.astype(BF16), cv_ref[hd].astype(BF16)) + _dot(p2.astype(BF16), vn_ref[:, sl])
        o_sc[:, sl] = o / l
    o_ref[...] = (_rms(o_sc[...]) * gatt_ref[...]).astype(BF16)


def _attn_step(qa, kt_new, va, ck_t, cv_t, layer, gext, gatt):
    _, nb, nh, dh, l_cache = ck_t.shape
    s_new = qa.shape[0] // nb
    tok = pl.BlockSpec((s_new, WIDTH_A), lambda b: (b, 0))
    cache = pl.BlockSpec((None, None, nh, dh, l_cache), lambda b: (layer, b, 0, 0, 0))
    return pl.pallas_call(
        _attn_step_body, grid=(nb,),
        in_specs=[tok, pl.BlockSpec((None, WIDTH_A, s_new), lambda b: (b, 0, 0)), tok, cache, cache,
                  pl.BlockSpec(gext.shape, lambda b: (0, 0)), pl.BlockSpec(gatt.shape, lambda b: (0, 0))],
        out_specs=tok, out_shape=jax.ShapeDtypeStruct(qa.shape, BF16),
        scratch_shapes=[pltpu.VMEM((s_new, WIDTH_A), F32)],
        compiler_params=_params(1), name="attn_step",
    )(qa, kt_new, va, ck_t, cv_t, gext, gatt)


def _gelu_tanh(x):
    return 0.5 * x * (1.0 + jnp.tanh(0.7978845608028654 * (x + 0.044715 * (x * x * x))))


def _shift_rows(u, prev_tile, k):
    tm, w = u.shape
    u3 = jnp.concatenate([prev_tile[None], u.reshape(tm // SUBLANES, SUBLANES, w)], axis=0)
    r = pltpu.roll(u3, k, 1)
    sub = lax.broadcasted_iota(jnp.int32, (1, SUBLANES, w), 1)
    return jnp.where(sub < k, r[:-1], r[1:]).reshape(tm, w)


def _ffn_body(x_ref, att_ref, ml_ref, ng_ref, wo_ref, wu_ref, wc_ref, bc_ref, wd_ref, init_ref,
              y_ref, tail_ref, carry_sc, a_sc, *, seq_rows):
    tm = x_ref.shape[0]
    d_ff = wd_ref.shape[0]
    multi_seq = seq_rows < tm

    if not multi_seq:
        @pl.when(pl.program_id(0) % (seq_rows // tm) == 0)
        def _seq_start():
            carry_sc[:SUBLANES - (CONV_W - 1), :] = jnp.zeros((SUBLANES - (CONV_W - 1), carry_sc.shape[1]), F32)
            carry_sc[SUBLANES - (CONV_W - 1):, :] = init_ref[...]

    mix = _dot(att_ref[...], wo_ref[:WIDTH_A, :]) + _dot(ml_ref[...], wo_ref[WIDTH_A:, :])
    x1 = x_ref[...] + _rms(mix) * ng_ref[1:2, :]
    h2 = (_rms(x1) * ng_ref[2:3, :]).astype(BF16)
    if multi_seq:
        pos = lax.broadcasted_iota(jnp.int32, (tm, FFN_COLS), 0) & (seq_rows - 1)

    def conv(cols):
        u = _dot(h2, wu_ref[:, cols])
        if multi_seq:
            fix = init_ref[:, cols]
            um2 = jnp.where(pos < 2, fix, pltpu.roll(u, 2, 0))
            um1 = jnp.where(pos < 1, pltpu.roll(fix, tm - 1, 0), pltpu.roll(u, 1, 0))
            tail_ref[:, cols] = u
        else:
            prev = carry_sc[:, cols]
            um2 = _shift_rows(u, prev, 2)
            um1 = _shift_rows(u, prev, 1)
            carry_sc[:, cols] = u[tm - SUBLANES:, :]
            tail_ref[:, cols] = u[tm - (CONV_W - 1):, :]
        return bc_ref[:, cols] + um2 * wc_ref[0:1, cols] + um1 * wc_ref[1:2, cols] + u * wc_ref[2:3, cols]

    for c in range(d_ff // FFN_COLS):
        gate = conv(slice(c * FFN_COLS, (c + 1) * FFN_COLS))
        up = conv(slice(d_ff + c * FFN_COLS, d_ff + (c + 1) * FFN_COLS))
        a_sc[:, c * FFN_COLS:(c + 1) * FFN_COLS] = (_gelu_tanh(gate) * up).astype(BF16)
    f = _dot(a_sc[...], wd_ref[...])
    y_ref[...] = x1 + _rms(f) * ng_ref[3:4, :]


def _ffn(x2d, att, ml, ng, wo, wu, wc, bc, wd, init, seq_rows):
    t, d = x2d.shape
    d_ff = wd.shape[0]
    tm = min(FFN_ROWS, t)
    assert t % tm == 0 and d_ff % FFN_COLS == 0
    multi_seq = seq_rows < tm
    if multi_seq:
        assert tm % seq_rows == 0 and t == tm and seq_rows & (seq_rows - 1) == 0
        init_spec = pl.BlockSpec(init.shape, lambda i: (0, 0))
        tail_spec = pl.BlockSpec((tm, 2 * d_ff), lambda i: (0, 0))
        tail_shape = jax.ShapeDtypeStruct((tm, 2 * d_ff), F32)
    else:
        assert seq_rows % tm == 0
        per_seq = seq_rows // tm
        init_spec = pl.BlockSpec((None, CONV_W - 1, 2 * d_ff), lambda i: (i // per_seq, 0, 0))
        tail_spec = init_spec
        tail_shape = jax.ShapeDtypeStruct(init.shape, F32)
    row_spec = lambda w: pl.BlockSpec((tm, w), lambda i: (i, 0))
    return pl.pallas_call(
        functools.partial(_ffn_body, seq_rows=seq_rows), grid=(t // tm,),
        in_specs=[row_spec(d), row_spec(WIDTH_A), row_spec(WIDTH_B), _const_spec(ng.shape),
                  _const_spec(wo.shape), _const_spec(wu.shape), _const_spec(wc.shape),
                  _const_spec(bc.shape), _const_spec(wd.shape), init_spec],
        out_specs=[row_spec(d), tail_spec],
        out_shape=[jax.ShapeDtypeStruct((t, d), F32), tail_shape],
        scratch_shapes=[pltpu.VMEM((SUBLANES, 2 * d_ff), F32), pltpu.VMEM((tm, d_ff), BF16)],
        compiler_params=_params(1), name="ffn",
    )(x2d, att, ml, ng, wo, wu, wc, bc, wd, init)


def _bias_rows(rel_table_l):
    n_far = ATT_WINDOW - REL_CLIP + 1
    far = jnp.broadcast_to(rel_table_l[:, 2 * REL_CLIP:], (N_HEADS_A, n_far))
    lo = ATT_WINDOW + REL_CLIP - (ATT_BAND - 1)
    near = rel_table_l[:, lo:2 * REL_CLIP][:, ::-1]
    wrap = jnp.broadcast_to(rel_table_l[:, 2 * REL_CLIP:], (N_HEADS_A, BIAS_W - ATT_BAND))
    return jnp.concatenate([far, near, wrap], axis=1)


def _pad_seq(a, seq_pad):
    return a if a.shape[1] == seq_pad else jnp.pad(a, ((0, 0), (0, seq_pad - a.shape[1]), (0, 0)))


def _pad_cols(a, nb, seq, seq_pad):
    if seq == seq_pad:
        return a
    return jnp.pad(a.reshape(-1, nb, seq), ((0, 0), (0, 0), (0, seq_pad - seq))).reshape(a.shape[0], -1)


def _layer(x, k_cache, v_cache, c0, n0, m0, conv_buf, w):
    nb, seq, d = x.shape
    x2d = x.reshape(nb * seq, d)
    blk = MLSTM_BLOCK if seq % MLSTM_BLOCK == 0 else LANES
    seg = min(blk, seq)
    qa, va, qb, vb, ob, kt, gcb = _inproj(x2d, w["g_in"], w["w_main"], w["w_kt"], w["gate_bias"], seg)

    keep = min(ATT_WINDOW, seq) if k_cache is None else seq
    k_tail = jnp.stack([kt[:WIDTH_A, (b + 1) * seq - keep:(b + 1) * seq] for b in range(nb)])
    if k_cache is None:
        att = _attn_prompt(qa, kt, va, w["gext"], w["g_att"], nb)
    else:
        att = _attn_step(qa, k_tail, va, k_cache, v_cache, w["layer"], w["gext"], w["g_att"])
    new_k = k_tail.reshape(nb, N_HEADS_A, HEAD_DIM_A, keep).astype(F32).swapaxes(-1, -2)
    new_v = va.reshape(nb, seq, WIDTH_A)[:, seq - keep:].reshape(nb, keep, N_HEADS_A, HEAD_DIM_A)
    new_v = new_v.transpose(0, 2, 1, 3).astype(F32)

    seq_pad = -(-seq // blk) * blk
    dh = HEAD_DIM_B
    c_ext0 = jnp.concatenate([c0, n0[..., None], jnp.zeros(c0.shape[:3] + (dh - 1,), F32)], axis=-1)
    m0b = jnp.broadcast_to(m0.reshape(-1, 1), (nb * N_HEADS_B, LANES))
    r3 = lambda a: _pad_seq(a.reshape(nb, seq, -1), seq_pad)
    hb, c_ext, m_b = _mlstm(r3(qb), _pad_cols(kt, nb, seq, seq_pad), r3(vb), r3(ob),
                            _pad_cols(gcb, nb, seq, seq_pad), w["g_mlstm"], c_ext0, m0b, blk,
                            seq - (seq_pad - blk))
    hb = hb[:, :seq].reshape(nb * seq, WIDTH_B)
    c1, n1, m1 = c_ext[..., :dh], c_ext[..., dh], m_b[:, 0].reshape(nb, N_HEADS_B)

    tm = min(FFN_ROWS, nb * seq)
    if seq < tm:
        init = jnp.pad(conv_buf, ((0, 0), (0, seq - (CONV_W - 1)), (0, 0))).reshape(nb * seq, -1)
        y, u = _ffn(x2d, att, hb, w["norm_g"], w["w_out"], w["w_up"], w["w_conv"], w["b_conv"],
                    w["w_down"], init, seq)
        new_buf = u.reshape(nb, seq, -1)[:, seq - (CONV_W - 1):]
    else:
        y, new_buf = _ffn(x2d, att, hb, w["norm_g"], w["w_out"], w["w_up"], w["w_conv"], w["b_conv"],
                          w["w_down"], conv_buf, seq)
    return y.reshape(nb, seq, d), new_k, new_v, c1, n1, m1, new_buf


def _layer_weights(l, norm_g, w_in, b_i, b_f, rel_table, g_att, g_mlstm, w_out, w_up, w_conv, b_conv, w_down):
    wa, wb = WIDTH_A, WIDTH_B
    wi = w_in[l]
    col = lambda k: slice(k * wa, (k + 1) * wa)
    assert wa == wb
    w_main = jnp.concatenate([wi[:, col(0)], wi[:, col(2)], wi[:, col(3)], wi[:, col(5)], wi[:, col(6)]],
                             axis=1).astype(BF16)
    wg = wi[:, 7 * wa:]
    wg_hi = wg.astype(BF16)
    wg_lo = (wg - wg_hi.astype(F32)).astype(BF16)
    w_kt = jnp.concatenate([wi[:, col(1)].astype(BF16), wi[:, col(4)].astype(BF16), wg_hi, wg_lo], axis=1).T
    return dict(
        layer=l, g_in=norm_g[l, 0:1], norm_g=norm_g[l], w_main=w_main, w_kt=w_kt,
        gate_bias=jnp.concatenate([b_i[l], b_f[l]])[:, None], gext=_bias_rows(rel_table[l]),
        g_att=g_att[l][None, :], g_mlstm=g_mlstm[l][None, :], w_out=w_out[l].astype(BF16),
        w_up=w_up[l].astype(BF16), w_conv=w_conv[l], b_conv=b_conv[l][None, :], w_down=w_down[l].astype(BF16))


def kernel(x_prompt, x_sample, cache_k_att, cache_v_att, state_mlstm_c, state_mlstm_n, state_mlstm_m,
           cache_ffn_conv, norm_g, w_in, b_i, b_f, rel_table, g_att, g_mlstm, w_out, w_up, w_conv,
           b_conv, w_down):
    depth = w_in.shape[0]
    bp = x_prompt.shape[0]
    d_ff2 = w_up.shape[2]
    dh = HEAD_DIM_B
    c_zero = jnp.zeros((bp, N_HEADS_B, dh, dh), F32)
    n_zero = jnp.zeros((bp, N_HEADS_B, dh), F32)
    m_zero = jnp.zeros((bp, N_HEADS_B), F32)
    buf_zero = jnp.zeros((bp, CONV_W - 1, d_ff2), F32)
    xp, xs = x_prompt, x_sample
    ck_t = cache_k_att.swapaxes(-1, -2)
    cv_t = cache_v_att.swapaxes(-1, -2)
    outs_p, outs_s = [], []
    for l in range(depth):
        w = _layer_weights(l, norm_g, w_in, b_i, b_f, rel_table, g_att, g_mlstm, w_out, w_up, w_conv,
                           b_conv, w_down)
        xp, *st_p = _layer(xp, None, None, c_zero, n_zero, m_zero, buf_zero, w)
        xs, *st_s = _layer(xs, ck_t, cv_t, state_mlstm_c[l], state_mlstm_n[l],
                           state_mlstm_m[l], cache_ffn_conv[l], w)
        outs_p.append(st_p)
        outs_s.append(st_s)
    stack = lambda outs, i: jnp.stack([o[i] for o in outs])
    return (xp, xs) + tuple(stack(outs_p, i) for i in range(6)) + tuple(stack(outs_s, i) for i in range(6))
```

```python
import functools
import math

import jax
import jax.numpy as jnp
from jax import lax
from jax.experimental import pallas as pl
from jax.experimental.pallas import tpu as pltpu

F32 = jnp.float32
BF16 = jnp.bfloat16

CHUNK = 64
LEFT_CHUNKS = 8
ATT_WINDOW = LEFT_CHUNKS * CHUNK
REL_CLIP = 256
N_HEADS_A = 8
HEAD_DIM_A = 64
WIDTH_A = N_HEADS_A * HEAD_DIM_A
N_HEADS_B = 4
HEAD_DIM_B = 128
WIDTH_B = N_HEADS_B * HEAD_DIM_B
CONV_W = 3
EPS = 1e-6
NEG = -1e30
LOG2E = math.log2(math.e)

LANES = 128
SUBLANES = 8
VMEM_LIMIT_BYTES = 56 * 1024 * 1024

INPROJ_ROWS = 512
ATT_ROWS = ATT_WINDOW
ATT_GROUP = 4 * CHUNK
ATT_BAND = ATT_GROUP + ATT_WINDOW
BIAS_W = 1024
PAIR_W = 2 * HEAD_DIM_A
MLSTM_BLOCK = 256
MLSTM_BATCHES = 2
FFN_ROWS = 512
FFN_COLS = 256


def _params(n_axes):
    return pltpu.CompilerParams(dimension_semantics=("arbitrary",) * n_axes,
                                vmem_limit_bytes=VMEM_LIMIT_BYTES)


def _layer_spec(stacked, layer):
    rest = (0,) * (stacked.ndim - 1)
    return pl.BlockSpec((None,) + stacked.shape[1:], lambda *_: (layer,) + rest, pipeline_mode=pl.Buffered(1))


def _rms(x):
    return x * lax.rsqrt(jnp.mean(x * x, axis=-1, keepdims=True) + EPS)


def _dot(a, b):
    return jnp.dot(a, b, preferred_element_type=F32)


def _dot_nt(a, b):
    return lax.dot_general(a, b, (((1,), (1,)), ((), ())), preferred_element_type=F32)


def _inproj_body(x_ref, g_ref, wm_ref, wkt_ref, gb_ref,
                 qa_ref, va_ref, qb_ref, vb_ref, ob_ref, kt_ref, gcb_ref, *, seg):
    n_keys = WIDTH_A + WIDTH_B
    hb = (_rms(x_ref[...]) * g_ref[0:1, :]).astype(BF16)
    zk = _dot_nt(wkt_ref[...], hb)
    kt_ref[:WIDTH_A, :] = zk[:WIDTH_A].astype(BF16)
    kt_ref[WIDTH_A:, :] = (zk[WIDTH_A:n_keys] * (HEAD_DIM_B ** -0.5)).astype(BF16)
    pre = zk[n_keys:n_keys + 2 * N_HEADS_B] + zk[n_keys + 2 * N_HEADS_B:] + gb_ref[...]
    lf = jnp.minimum(pre, 0.0) - jnp.log(1.0 + jnp.exp(-jnp.abs(pre)))
    pos = lax.broadcasted_iota(jnp.int32, pre.shape, 1) & (seg - 1)
    k = 1
    while k < seg:
        lf = lf + jnp.where(pos >= k, pltpu.roll(lf, k, 1), 0.0)
        k *= 2
    row = lax.broadcasted_iota(jnp.int32, pre.shape, 0)
    gcb_ref[...] = jnp.where(row < N_HEADS_B, pre, lf)
    outs = (qa_ref, va_ref, qb_ref, vb_ref, ob_ref)
    for c, o_ref in enumerate(outs):
        z = _dot(hb, wm_ref[:, c * WIDTH_A:(c + 1) * WIDTH_A])
        if c == 0:
            z = z * (HEAD_DIM_A ** -0.5 * LOG2E)
        o_ref[...] = z.astype(BF16)


def _inproj(x2d, layer, ng, wm, wkt, gb, seg):
    t, d = x2d.shape
    tm = min(INPROJ_ROWS, t)
    assert t % tm == 0 and tm % seg == 0 and seg & (seg - 1) == 0
    row_spec = lambda w: pl.BlockSpec((tm, w), lambda i: (i, 0))
    col_spec = lambda r: pl.BlockSpec((r, tm), lambda i: (0, i))
    out_shape = [jax.ShapeDtypeStruct((t, WIDTH_A), BF16)] * 5 + [
        jax.ShapeDtypeStruct((WIDTH_A + WIDTH_B, t), BF16), jax.ShapeDtypeStruct((2 * N_HEADS_B, t), F32)]
    out_specs = [row_spec(WIDTH_A)] * 5 + [col_spec(WIDTH_A + WIDTH_B), col_spec(2 * N_HEADS_B)]
    return pl.pallas_call(
        functools.partial(_inproj_body, seg=seg), grid=(t // tm,),
        in_specs=[row_spec(d)] + [_layer_spec(a, layer) for a in (ng, wm, wkt, gb)],
        out_specs=out_specs, out_shape=out_shape, compiler_params=_params(1), name="inproj",
    )(x2d, ng, wm, wkt, gb)


def _mlstm_body(*refs, n_valid, nbs):
    q_ref, v_ref, o_ref = refs[:3]
    kt_refs = refs[3:3 + nbs]
    gcb_refs = refs[3 + nbs:3 + 2 * nbs]
    g_ref, c0_ref, m0_ref, h_ref, c_out_ref, m_out_ref, c_sc, m_sc, vext_sc = refs[3 + 2 * nbs:]
    j = pl.program_id(1)
    blk = q_ref.shape[1]
    dh = HEAD_DIM_B

    @pl.when(j == 0)
    def _load_state():
        c_sc[...] = c0_ref[...]
        m_sc[...] = m0_ref[...]
        lane = lax.broadcasted_iota(jnp.int32, (blk, dh), 1)
        ones_col = jnp.where(lane == 0, 1.0, 0.0).astype(BF16)
        for s in range(nbs * N_HEADS_B):
            vext_sc[s, :, dh:] = ones_col

    row = lax.broadcasted_iota(jnp.int32, (blk, blk), 0)
    col = lax.broadcasted_iota(jnp.int32, (blk, blk), 1)
    lane1 = lax.broadcasted_iota(jnp.int32, (1, blk), 1)
    for bi in range(nbs):
        for hd in range(N_HEADS_B):
            s_idx = bi * N_HEADS_B + hd
            sl = slice(hd * dh, (hd + 1) * dh)
            q = q_ref[bi, :, sl]
            kt = kt_refs[bi][sl, :]
            vext_sc[s_idx, :, :dh] = v_ref[bi, :, sl]
            vext = vext_sc[s_idx]
            ig_r = gcb_refs[bi][hd:hd + 1, :]
            b_r = gcb_refs[bi][N_HEADS_B + hd:N_HEADS_B + hd + 1, :]
            if n_valid < blk:
                ig_r = jnp.where(lane1 < n_valid, ig_r, NEG)
                b_r = jnp.where(lane1 < n_valid, b_r, b_r[:, n_valid - 1:n_valid])
            m_prev = m_sc[s_idx:s_idx + 1, 0:1]
            b_c = jnp.sum(jnp.where(row == col, b_r, 0.0), axis=1, keepdims=True)
            dmat = jnp.where(col <= row, b_c - b_r + ig_r, NEG)
            inter = b_c + m_prev
            m_t = jnp.maximum(inter, jnp.max(dmat, axis=1, keepdims=True))
            w_intra = jnp.exp(dmat - m_t)
            w_inter = jnp.exp(inter - m_t)
            c_old = c_sc[bi, hd]
            qkc = _dot(q, jnp.concatenate([kt, c_old.astype(BF16)], axis=1))
            s = (qkc[:, :blk] * w_intra).astype(BF16)
            m_new = m_t[blk - 1:blk, :]
            b_last = b_r[:, blk - 1:blk]
            w_s = jnp.exp(b_last - b_r + ig_r - m_new)
            decay = jnp.exp(b_last + m_prev - m_new)
            ktw = (kt.astype(F32) * w_s).astype(BF16)
            sv = _dot(jnp.concatenate([s, ktw], axis=0), vext)
            ext = sv[:blk] + w_inter * qkc[:, blk:]
            hh = ext[:, :dh] / jnp.maximum(jnp.abs(ext[:, dh:dh + 1]), jnp.exp(-m_t))
            hh = _rms(hh)
            gate = 1.0 / (1.0 + jnp.exp(-o_ref[bi, :, sl].astype(F32)))
            h_ref[bi, :, sl] = (hh * g_ref[:, sl] * gate).astype(BF16)
            c_sc[bi, hd] = decay * c_old + sv[blk:]
            m_sc[s_idx:s_idx + 1, :] = jnp.broadcast_to(m_new, (1, m_sc.shape[1]))

    @pl.when(j == pl.num_programs(1) - 1)
    def _store_state():
        c_out_ref[...] = c_sc[...]
        m_out_ref[...] = m_sc[...]


def _mlstm(q, kt, v, o, gcb, layer, g, c0, m0, blk, n_valid):
    nb, seq, _ = q.shape
    nblk = seq // blk
    nbs = math.gcd(nb, MLSTM_BATCHES)
    dh = HEAD_DIM_B
    tok = pl.BlockSpec((nbs, blk, WIDTH_B), lambda bg, j: (bg, j, 0))
    kt_specs = [pl.BlockSpec((WIDTH_B, blk), lambda bg, j, k=k: (WIDTH_A // WIDTH_B, (bg * nbs + k) * nblk + j))
                for k in range(nbs)]
    gcb_specs = [pl.BlockSpec((2 * N_HEADS_B, blk), lambda bg, j, k=k: (0, (bg * nbs + k) * nblk + j))
                 for k in range(nbs)]
    cspec = pl.BlockSpec((nbs, N_HEADS_B, dh, 2 * dh), lambda bg, j: (bg, 0, 0, 0))
    mspec = pl.BlockSpec((nbs * N_HEADS_B, LANES), lambda bg, j: (bg, 0))
    return pl.pallas_call(
        functools.partial(_mlstm_body, n_valid=n_valid, nbs=nbs), grid=(nb // nbs, nblk),
        in_specs=[tok, tok, tok] + kt_specs + gcb_specs + [_layer_spec(g, layer), cspec, mspec],
        out_specs=[tok, cspec, mspec],
        out_shape=[jax.ShapeDtypeStruct(q.shape, BF16), jax.ShapeDtypeStruct(c0.shape, F32),
                   jax.ShapeDtypeStruct(m0.shape, F32)],
        scratch_shapes=[pltpu.VMEM((nbs, N_HEADS_B, dh, 2 * dh), F32), pltpu.VMEM((nbs * N_HEADS_B, LANES), F32),
                        pltpu.VMEM((nbs * N_HEADS_B, blk, 2 * dh), BF16)],
        compiler_params=_params(2), name="mlstm",
    )(q, v, o, *([kt] * nbs), *([gcb] * nbs), g, c0, m0)


def _toeplitz_bias(gext_row, rows):
    gx = jnp.broadcast_to(gext_row, (rows, BIAS_W))
    return pltpu.roll(gx, 0, 1, stride=1, stride_axis=0)


def _attn_prompt_body(q_ref, kp_ref, kc_ref, vp_ref, vc_ref, gext_ref, gatt_ref, o_ref, bias_sc, o_sc):
    b = pl.program_id(0)
    i = pl.program_id(1)
    n_pairs = N_HEADS_A // 2

    @pl.when((b == 0) & (i == 0))
    def _init_bias():
        r = lax.broadcasted_iota(jnp.int32, (ATT_GROUP, BIAS_W), 0)
        c = lax.broadcasted_iota(jnp.int32, (ATT_GROUP, BIAS_W), 1)
        first = r - (r & (CHUNK - 1))
        ok = (c >= first) & (c < first + ATT_WINDOW + CHUNK)
        for hd in range(N_HEADS_A):
            t = _toeplitz_bias(gext_ref[hd:hd + 1, :], ATT_GROUP) * LOG2E
            bias_sc[hd // 2, (hd % 2) * ATT_GROUP:(hd % 2 + 1) * ATT_GROUP, :] = jnp.where(ok, t, NEG)[:, :ATT_BAND]

    lane = lax.broadcasted_iota(jnp.int32, (ATT_GROUP, PAIR_W), 1)
    low = lane < HEAD_DIM_A

    def step(has_prev):
        for g in range(ATT_ROWS // ATT_GROUP):
            r0 = g * ATT_GROUP
            n_prev = ATT_WINDOW - r0 if has_prev else 0
            n_cur = r0 + ATT_GROUP
            for p in range(n_pairs):
                ps = slice(p * PAIR_W, (p + 1) * PAIR_W)
                qp = q_ref[r0:r0 + ATT_GROUP, ps]
                zero = jnp.zeros_like(qp)
                qq = jnp.concatenate([jnp.where(low, qp, zero), jnp.where(low, zero, qp)], axis=0)
                s_cur = _dot(qq, kc_ref[ps, :n_cur])
                bias = bias_sc[p]
                if has_prev:
                    s = jnp.concatenate([_dot(qq, kp_ref[ps, r0:]), s_cur], axis=1) + bias
                else:
                    s = s_cur + bias[:, ATT_BAND - n_cur:]
                e = jnp.exp2(s - jnp.max(s, axis=-1, keepdims=True))
                l = jnp.sum(e, axis=-1, keepdims=True)
                eb = e.astype(BF16)
                o = _dot(eb[:, n_prev:], vc_ref[:n_cur, ps])
                if has_prev:
                    o = o + _dot(eb[:, :n_prev], vp_ref[r0:, ps])
                o = o / l
                o_sc[:, ps] = jnp.where(low, o[:ATT_GROUP], o[ATT_GROUP:])
            o_ref[r0:r0 + ATT_GROUP, :] = (_rms(o_sc[...]) * gatt_ref[...]).astype(BF16)

    @pl.when(i == 0)
    def _first_block():
        step(False)

    @pl.when(i > 0)
    def _later_block():
        step(True)


def _attn_prompt(qa, kt, va, layer, gext, gatt, nb):
    t = qa.shape[0]
    nblk = t // (nb * ATT_ROWS)
    cur = pl.BlockSpec((ATT_ROWS, WIDTH_A), lambda b, i: (b * nblk + i, 0))
    prev = pl.BlockSpec((ATT_ROWS, WIDTH_A), lambda b, i: (b * nblk + jnp.maximum(i - 1, 0), 0))
    cur_t = pl.BlockSpec((WIDTH_A, ATT_ROWS), lambda b, i: (0, b * nblk + i))
    prev_t = pl.BlockSpec((WIDTH_A, ATT_ROWS), lambda b, i: (0, b * nblk + jnp.maximum(i - 1, 0)))
    return pl.pallas_call(
        _attn_prompt_body, grid=(nb, nblk),
        in_specs=[cur, prev_t, cur_t, prev, cur, _layer_spec(gext, layer), _layer_spec(gatt, layer)],
        out_specs=cur, out_shape=jax.ShapeDtypeStruct((t, WIDTH_A), BF16),
        scratch_shapes=[pltpu.VMEM((N_HEADS_A // 2, 2 * ATT_GROUP, ATT_BAND), F32),
                        pltpu.VMEM((ATT_GROUP, WIDTH_A), F32)],
        compiler_params=_params(2), name="attn_prompt",
    )(qa, kt, kt, va, va, gext, gatt)


def _attn_step_body(q_ref, kn_ref, vn_ref, ck_ref, cv_ref, gext_ref, gatt_ref, o_ref, o_sc):
    s_new = q_ref.shape[0]
    l_cache = ck_ref.shape[2]
    for hd in range(N_HEADS_A):
        sl = slice(hd * HEAD_DIM_A, (hd + 1) * HEAD_DIM_A)
        bias = _toeplitz_bias(gext_ref[hd:hd + 1, :], s_new) * LOG2E
        q = q_ref[:, sl]
        s1 = _dot(q, ck_ref[hd].astype(BF16)) + bias[:, :l_cache]
        s2 = _dot(q, kn_ref[sl, :]) + bias[:, l_cache:l_cache + s_new]
        m = jnp.maximum(jnp.max(s1, axis=-1, keepdims=True), jnp.max(s2, axis=-1, keepdims=True))
        p1 = jnp.exp2(s1 - m)
        p2 = jnp.exp2(s2 - m)
        l = jnp.sum(p1, axis=-1, keepdims=True) + jnp.sum(p2, axis=-1, keepdims=True)
        o = _dot_nt(p1.astype(BF16), cv_ref[hd].astype(BF16)) + _dot(p2.astype(BF16), vn_ref[:, sl])
        o_sc[:, sl] = o / l
    o_ref[...] = (_rms(o_sc[...]) * gatt_ref[...]).astype(BF16)


def _attn_step(qa, kt_new, va, ck_t, cv_t, layer, gext, gatt):
    _, nb, nh, dh, l_cache = ck_t.shape
    s_new = qa.shape[0] // nb
    tok = pl.BlockSpec((s_new, WIDTH_A), lambda b: (b, 0))
    cache = pl.BlockSpec((None, None, nh, dh, l_cache), lambda b: (layer, b, 0, 0, 0))
    return pl.pallas_call(
        _attn_step_body, grid=(nb,),
        in_specs=[tok, pl.BlockSpec((None, WIDTH_A, s_new), lambda b: (b, 0, 0)), tok, cache, cache,
                  _layer_spec(gext, layer), _layer_spec(gatt, layer)],
        out_specs=tok, out_shape=jax.ShapeDtypeStruct(qa.shape, BF16),
        scratch_shapes=[pltpu.VMEM((s_new, WIDTH_A), F32)],
        compiler_params=_params(1), name="attn_step",
    )(qa, kt_new, va, ck_t, cv_t, gext, gatt)


GELU_C = math.sqrt(2.0 / math.pi)
GELU_A = 0.044715


def _gelu_gate(x, up):
    e = jnp.exp2(x * ((-2.0 * GELU_C * GELU_A * LOG2E) * (x * x) + (-2.0 * GELU_C * LOG2E)))
    return (x * up) / (1.0 + e)


def _shift_rows(u, prev_tile, k):
    tm, w = u.shape
    u3 = jnp.concatenate([prev_tile[None], u.reshape(tm // SUBLANES, SUBLANES, w)], axis=0)
    r = pltpu.roll(u3, k, 1)
    sub = lax.broadcasted_iota(jnp.int32, (1, SUBLANES, w), 1)
    return jnp.where(sub < k, r[:-1], r[1:]).reshape(tm, w)


def _ffn_body(x_ref, att_ref, ml_ref, ng_ref, wo_ref, wu_ref, wc_ref, bc_ref, wd_ref, init_ref,
              y_ref, tail_ref, carry_sc, a_sc, *, seq_rows):
    tm = x_ref.shape[0]
    d_ff = wd_ref.shape[0]
    multi_seq = seq_rows < tm

    if not multi_seq:
        @pl.when(pl.program_id(0) % (seq_rows // tm) == 0)
        def _seq_start():
            carry_sc[:SUBLANES - (CONV_W - 1), :] = jnp.zeros((SUBLANES - (CONV_W - 1), carry_sc.shape[1]), F32)
            carry_sc[SUBLANES - (CONV_W - 1):, :] = init_ref[...]

    mix = _dot(att_ref[...], wo_ref[:WIDTH_A, :]) + _dot(ml_ref[...], wo_ref[WIDTH_A:, :])
    x1 = x_ref[...] + _rms(mix) * ng_ref[1:2, :]
    h2 = (_rms(x1) * ng_ref[2:3, :]).astype(BF16)
    if multi_seq:
        pos = lax.broadcasted_iota(jnp.int32, (tm, FFN_COLS), 0) & (seq_rows - 1)

    def conv(cols):
        u = _dot(h2, wu_ref[:, cols])
        if multi_seq:
            fix = init_ref[:, cols]
            um2 = jnp.where(pos < 2, fix, pltpu.roll(u, 2, 0))
            um1 = jnp.where(pos < 1, pltpu.roll(fix, tm - 1, 0), pltpu.roll(u, 1, 0))
            tail_ref[:, cols] = u
        else:
            prev = carry_sc[:, cols]
            um2 = _shift_rows(u, prev, 2)
            um1 = _shift_rows(u, prev, 1)
            carry_sc[:, cols] = u[tm - SUBLANES:, :]
            tail_ref[:, cols] = u[tm - (CONV_W - 1):, :]
        return bc_ref[:, cols] + um2 * wc_ref[0:1, cols] + um1 * wc_ref[1:2, cols] + u * wc_ref[2:3, cols]

    for c in range(d_ff // FFN_COLS):
        gate = conv(slice(c * FFN_COLS, (c + 1) * FFN_COLS))
        up = conv(slice(d_ff + c * FFN_COLS, d_ff + (c + 1) * FFN_COLS))
        a_sc[:, c * FFN_COLS:(c + 1) * FFN_COLS] = _gelu_gate(gate, up).astype(BF16)
    f = _dot(a_sc[...], wd_ref[...])
    y_ref[...] = x1 + _rms(f) * ng_ref[3:4, :]


def _ffn(x2d, att, ml, layer, ng, wo, wu, wc, bc, wd, init, seq_rows):
    t, d = x2d.shape
    d_ff = wd.shape[1]
    tm = min(FFN_ROWS, t)
    assert t % tm == 0 and d_ff % FFN_COLS == 0
    multi_seq = seq_rows < tm
    if multi_seq:
        assert tm % seq_rows == 0 and t == tm and seq_rows & (seq_rows - 1) == 0
        init_spec = pl.BlockSpec(init.shape, lambda i: (0, 0))
        tail_spec = pl.BlockSpec((tm, 2 * d_ff), lambda i: (0, 0))
        tail_shape = jax.ShapeDtypeStruct((tm, 2 * d_ff), F32)
    else:
        assert seq_rows % tm == 0
        per_seq = seq_rows // tm
        init_spec = pl.BlockSpec((None, CONV_W - 1, 2 * d_ff), lambda i: (i // per_seq, 0, 0))
        tail_spec = init_spec
        tail_shape = jax.ShapeDtypeStruct(init.shape, F32)
    row_spec = lambda w: pl.BlockSpec((tm, w), lambda i: (i, 0))
    return pl.pallas_call(
        functools.partial(_ffn_body, seq_rows=seq_rows), grid=(t // tm,),
        in_specs=([row_spec(d), row_spec(WIDTH_A), row_spec(WIDTH_B)]
                  + [_layer_spec(a, layer) for a in (ng, wo, wu, wc, bc, wd)] + [init_spec]),
        out_specs=[row_spec(d), tail_spec],
        out_shape=[jax.ShapeDtypeStruct((t, d), F32), tail_shape],
        scratch_shapes=[pltpu.VMEM((SUBLANES, 2 * d_ff), F32), pltpu.VMEM((tm, d_ff), BF16)],
        compiler_params=_params(1), name="ffn",
    )(x2d, att, ml, ng, wo, wu, wc, bc, wd, init)


def _bias_rows(rel_table_l):
    n_far = ATT_WINDOW - REL_CLIP + 1
    far = jnp.broadcast_to(rel_table_l[:, 2 * REL_CLIP:], (N_HEADS_A, n_far))
    lo = ATT_WINDOW + REL_CLIP - (ATT_BAND - 1)
    near = rel_table_l[:, lo:2 * REL_CLIP][:, ::-1]
    wrap = jnp.broadcast_to(rel_table_l[:, 2 * REL_CLIP:], (N_HEADS_A, BIAS_W - ATT_BAND))
    return jnp.concatenate([far, near, wrap], axis=1)


def _pad_seq(a, seq_pad):
    return a if a.shape[1] == seq_pad else jnp.pad(a, ((0, 0), (0, seq_pad - a.shape[1]), (0, 0)))


def _pad_cols(a, nb, seq, seq_pad):
    if seq == seq_pad:
        return a
    return jnp.pad(a.reshape(-1, nb, seq), ((0, 0), (0, 0), (0, seq_pad - seq))).reshape(a.shape[0], -1)


def _layer(x, k_cache, v_cache, c0, n0, m0, conv_buf, w):
    nb, seq, d = x.shape
    x2d = x.reshape(nb * seq, d)
    blk = MLSTM_BLOCK if seq % MLSTM_BLOCK == 0 else LANES
    seg = min(blk, seq)
    l = w["layer"]
    qa, va, qb, vb, ob, kt, gcb = _inproj(x2d, l, w["norm_g"], w["w_main"], w["w_kt"], w["gate_bias"], seg)

    keep = min(ATT_WINDOW, seq) if k_cache is None else seq
    k_tail = jnp.stack([kt[:WIDTH_A, (b + 1) * seq - keep:(b + 1) * seq] for b in range(nb)])
    if k_cache is None:
        att = _attn_prompt(qa, kt, va, l, w["gext"], w["g_att"], nb)
    else:
        att = _attn_step(qa, k_tail, va, k_cache, v_cache, l, w["gext"], w["g_att"])
    new_k = k_tail.reshape(nb, N_HEADS_A, HEAD_DIM_A, keep).astype(F32).swapaxes(-1, -2)
    new_v = va.reshape(nb, seq, WIDTH_A)[:, seq - keep:].reshape(nb, keep, N_HEADS_A, HEAD_DIM_A)
    new_v = new_v.transpose(0, 2, 1, 3).astype(F32)

    seq_pad = -(-seq // blk) * blk
    dh = HEAD_DIM_B
    c_ext0 = jnp.concatenate([c0, n0[..., None], jnp.zeros(c0.shape[:3] + (dh - 1,), F32)], axis=-1)
    m0b = jnp.broadcast_to(m0.reshape(-1, 1), (nb * N_HEADS_B, LANES))
    r3 = lambda a: _pad_seq(a.reshape(nb, seq, -1), seq_pad)
    hb, c_ext, m_b = _mlstm(r3(qb), _pad_cols(kt, nb, seq, seq_pad), r3(vb), r3(ob),
                            _pad_cols(gcb, nb, seq, seq_pad), l, w["g_mlstm"], c_ext0, m0b, blk,
                            seq - (seq_pad - blk))
    hb = hb[:, :seq].reshape(nb * seq, WIDTH_B)
    c1, n1, m1 = c_ext[..., :dh], c_ext[..., dh], m_b[:, 0].reshape(nb, N_HEADS_B)

    tm = min(FFN_ROWS, nb * seq)
    if seq < tm:
        init = jnp.pad(conv_buf, ((0, 0), (0, seq - (CONV_W - 1)), (0, 0))).reshape(nb * seq, -1)
        y, u = _ffn(x2d, att, hb, l, w["norm_g"], w["w_out"], w["w_up"], w["w_conv"], w["b_conv"],
                    w["w_down"], init, seq)
        new_buf = u.reshape(nb, seq, -1)[:, seq - (CONV_W - 1):]
    else:
        y, new_buf = _ffn(x2d, att, hb, l, w["norm_g"], w["w_out"], w["w_up"], w["w_conv"], w["b_conv"],
                          w["w_down"], conv_buf, seq)
    return y.reshape(nb, seq, d), new_k, new_v, c1, n1, m1, new_buf


def _stacked_weights(norm_g, w_in, b_i, b_f, rel_table, g_att, g_mlstm, w_out, w_up, w_conv, b_conv, w_down):
    wa, wb = WIDTH_A, WIDTH_B
    assert wa == wb
    col = lambda k: slice(k * wa, (k + 1) * wa)
    wi = w_in.astype(BF16)
    w_main = jnp.concatenate([wi[..., col(k)] for k in (0, 2, 3, 5, 6)], axis=-1)
    wg = w_in[..., 7 * wa:]
    wg_hi = wg.astype(BF16)
    wg_lo = (wg - wg_hi.astype(F32)).astype(BF16)
    w_kt = jnp.concatenate([wi[..., col(1)], wi[..., col(4)], wg_hi, wg_lo], axis=-1).swapaxes(1, 2)
    return dict(
        norm_g=norm_g, w_main=w_main, w_kt=w_kt, gate_bias=jnp.concatenate([b_i, b_f], axis=-1)[..., None],
        gext=jax.vmap(_bias_rows)(rel_table), g_att=g_att[:, None, :], g_mlstm=g_mlstm[:, None, :],
        w_out=w_out.astype(BF16), w_up=w_up.astype(BF16), w_conv=w_conv, b_conv=b_conv[:, None, :],
        w_down=w_down.astype(BF16))


def kernel(x_prompt, x_sample, cache_k_att, cache_v_att, state_mlstm_c, state_mlstm_n, state_mlstm_m,
           cache_ffn_conv, norm_g, w_in, b_i, b_f, rel_table, g_att, g_mlstm, w_out, w_up, w_conv,
           b_conv, w_down):
    depth = w_in.shape[0]
    bp = x_prompt.shape[0]
    d_ff2 = w_up.shape[2]
    dh = HEAD_DIM_B
    c_zero = jnp.zeros((bp, N_HEADS_B, dh, dh), F32)
    n_zero = jnp.zeros((bp, N_HEADS_B, dh), F32)
    m_zero = jnp.zeros((bp, N_HEADS_B), F32)
    buf_zero = jnp.zeros((bp, CONV_W - 1, d_ff2), F32)
    xp, xs = x_prompt, x_sample
    ck_t = cache_k_att.swapaxes(-1, -2)
    cv_t = cache_v_att.swapaxes(-1, -2)
    outs_p, outs_s = [], []
    stacks = _stacked_weights(norm_g, w_in, b_i, b_f, rel_table, g_att, g_mlstm, w_out, w_up, w_conv, b_conv,
                              w_down)
    for l in range(depth):
        w = dict(stacks, layer=l)
        xp, *st_p = _layer(xp, None, None, c_zero, n_zero, m_zero, buf_zero, w)
        xs, *st_s = _layer(xs, ck_t, cv_t, state_mlstm_c[l], state_mlstm_n[l],
                           state_mlstm_m[l], cache_ffn_conv[l], w)
        outs_p.append(st_p)
        outs_s.append(st_s)
    stack = lambda outs, i: jnp.stack([o[i] for o in outs])
    return (xp, xs) + tuple(stack(outs_p, i) for i in range(6)) + tuple(stack(outs_s, i) for i in range(6))
```

```python
import functools
import math

import jax
import jax.numpy as jnp
from jax import lax
from jax.experimental import pallas as pl
from jax.experimental.pallas import tpu as pltpu

F32 = jnp.float32
BF16 = jnp.bfloat16

CHUNK = 64
LEFT_CHUNKS = 8
ATT_WINDOW = LEFT_CHUNKS * CHUNK
REL_CLIP = 256
N_HEADS_A = 8
HEAD_DIM_A = 64
WIDTH_A = N_HEADS_A * HEAD_DIM_A
N_HEADS_B = 4
HEAD_DIM_B = 128
WIDTH_B = N_HEADS_B * HEAD_DIM_B
CONV_W = 3
EPS = 1e-6
NEG = -1e30
LOG2E = math.log2(math.e)

LANES = 128
SUBLANES = 8
VMEM_LIMIT_BYTES = 56 * 1024 * 1024

INPROJ_ROWS = 512
ATT_ROWS = ATT_WINDOW
ATT_GROUP = 4 * CHUNK
ATT_BAND = ATT_GROUP + ATT_WINDOW
BIAS_W = 1024
PAIR_W = 2 * HEAD_DIM_A
MLSTM_BLOCK = 256
MLSTM_BATCHES = 2
FFN_ROWS = 512
FFN_COLS = 256


def _params(n_axes):
    return pltpu.CompilerParams(dimension_semantics=("arbitrary",) * n_axes,
                                vmem_limit_bytes=VMEM_LIMIT_BYTES)


def _layer_spec(stacked, layer):
    rest = (0,) * (stacked.ndim - 1)
    return pl.BlockSpec((None,) + stacked.shape[1:], lambda *_: (layer,) + rest, pipeline_mode=pl.Buffered(1))


def _rms(x):
    return x * lax.rsqrt(jnp.mean(x * x, axis=-1, keepdims=True) + EPS)


def _dot(a, b):
    return jnp.dot(a, b, preferred_element_type=F32)


def _dot_nt(a, b):
    return lax.dot_general(a, b, (((1,), (1,)), ((), ())), preferred_element_type=F32)


def _inproj_body(x_ref, g_ref, wm_ref, wft_ref, gb_ref,
                 qa_ref, va_ref, kb_ref, ft_ref, gcb_ref, gcx_ref, *, seg):
    n_feat = ft_ref.shape[0]
    tm = x_ref.shape[0]
    hb = (_rms(x_ref[...]) * g_ref[0:1, :]).astype(BF16)
    zf = _dot_nt(wft_ref[...], hb)
    ft_ref[...] = zf[:n_feat].astype(BF16)
    pre = zf[n_feat:n_feat + 2 * N_HEADS_B] + zf[n_feat + 2 * N_HEADS_B:] + gb_ref[...]
    lf = jnp.minimum(pre, 0.0) - jnp.log(1.0 + jnp.exp(-jnp.abs(pre)))
    pos = lax.broadcasted_iota(jnp.int32, pre.shape, 1) & (seg - 1)
    k = 1
    while k < seg:
        lf = lf + jnp.where(pos >= k, pltpu.roll(lf, k, 1), 0.0)
        k *= 2
    row = lax.broadcasted_iota(jnp.int32, pre.shape, 0)
    gcb = jnp.where(row < N_HEADS_B, pre, lf)
    gcb_ref[...] = gcb
    c = gcb[:N_HEADS_B] - gcb[N_HEADS_B:]
    c1 = c.astype(BF16).astype(F32)
    c2 = (c - c1).astype(BF16).astype(F32)
    c3 = c - c1 - c2
    r8 = lax.broadcasted_iota(jnp.int32, (SUBLANES, tm), 0)
    tiles = [jnp.where(r8 == 0, c1[h:h + 1], jnp.where(r8 == 1, c2[h:h + 1], jnp.where(r8 == 2, c3[h:h + 1], 0.0)))
             for h in range(N_HEADS_B)]
    tiles.append(jnp.zeros((LANES - N_HEADS_B * SUBLANES, tm), F32))
    gcx_ref[...] = jnp.concatenate(tiles, axis=0).T.astype(BF16)
    for c_idx, (o_ref, scale) in enumerate(((qa_ref, HEAD_DIM_A ** -0.5 * LOG2E), (va_ref, None),
                                            (kb_ref, HEAD_DIM_B ** -0.5))):
        z = _dot(hb, wm_ref[:, c_idx * WIDTH_A:(c_idx + 1) * WIDTH_A])
        if scale is not None:
            z = z * scale
        o_ref[...] = z.astype(BF16)


def _inproj(x2d, layer, ng, wm, wft, gb, seg):
    t, d = x2d.shape
    tm = min(INPROJ_ROWS, t)
    assert t % tm == 0 and tm % seg == 0 and seg & (seg - 1) == 0
    n_feat = wft.shape[1] - 4 * N_HEADS_B
    row_spec = lambda w: pl.BlockSpec((tm, w), lambda i: (i, 0))
    col_spec = lambda r: pl.BlockSpec((r, tm), lambda i: (0, i))
    out_shape = [jax.ShapeDtypeStruct((t, WIDTH_A), BF16)] * 3 + [
        jax.ShapeDtypeStruct((n_feat, t), BF16), jax.ShapeDtypeStruct((2 * N_HEADS_B, t), F32),
        jax.ShapeDtypeStruct((t, LANES), BF16)]
    out_specs = [row_spec(WIDTH_A)] * 3 + [col_spec(n_feat), col_spec(2 * N_HEADS_B), row_spec(LANES)]
    return pl.pallas_call(
        functools.partial(_inproj_body, seg=seg), grid=(t // tm,),
        in_specs=[row_spec(d)] + [_layer_spec(a, layer) for a in (ng, wm, wft, gb)],
        out_specs=out_specs, out_shape=out_shape, compiler_params=_params(1), name="inproj",
    )(x2d, ng, wm, wft, gb)


def _mlstm_body(*refs, n_valid, nbs):
    k_ref, gcx_ref = refs[:2]
    ft_refs = refs[2:2 + 3 * nbs]
    gcb_refs = refs[2 + 3 * nbs:2 + 4 * nbs]
    (g_ref, c0_ref, n0_ref, m0_ref, h_ref, c_out_ref, n_out_ref, m_out_ref, ct_sc, m_sc) = refs[2 + 4 * nbs:]
    j = pl.program_id(1)
    blk = k_ref.shape[1]
    dh = HEAD_DIM_B

    @pl.when(j == 0)
    def _load_state():
        m_sc[...] = m0_ref[...]
        for bi in range(nbs):
            for hd in range(N_HEADS_B):
                ct_sc[bi, hd, :dh, :] = c0_ref[bi, hd].T
                ct_sc[bi, hd, dh:, :] = jnp.concatenate([n0_ref[bi, hd], jnp.zeros((dh - 1, dh), F32)], axis=0)

    row = lax.broadcasted_iota(jnp.int32, (blk, blk), 0)
    col = lax.broadcasted_iota(jnp.int32, (blk, blk), 1)
    valid = row <= col
    if n_valid < blk:
        valid = valid & (row < n_valid)
    lane1 = lax.broadcasted_iota(jnp.int32, (1, blk), 1)
    r128 = lax.broadcasted_iota(jnp.int32, (LANES, blk), 0)
    ones_tail = jnp.where(r128 == 0, 1.0, 0.0).astype(BF16)
    last = n_valid - 1
    for bi in range(nbs):
        qt_ref, vt_ref, ot_ref = ft_refs[3 * bi:3 * bi + 3]
        for hd in range(N_HEADS_B):
            s_idx = bi * N_HEADS_B + hd
            sl = slice(hd * dh, (hd + 1) * dh)
            qt = qt_ref[sl, :]
            kk = k_ref[bi, :, sl]
            ig_r = gcb_refs[bi][hd:hd + 1, :]
            b_r = gcb_refs[bi][N_HEADS_B + hd:N_HEADS_B + hd + 1, :]
            m_prev = m_sc[s_idx:s_idx + 1, 0:1]
            pick = jnp.where((r128 >= hd * SUBLANES) & (r128 < hd * SUBLANES + 3), 1.0, 0.0).astype(BF16)
            dmat = jnp.where(valid, _dot(gcx_ref[bi], pick) + b_r, NEG)
            inter = b_r + m_prev
            m_t = jnp.maximum(inter, jnp.max(dmat, axis=0, keepdims=True))
            w_intra = jnp.exp(dmat - m_t)
            w_inter = jnp.exp(inter - m_t)
            ct_old = ct_sc[bi, hd]
            kc = _dot(jnp.concatenate([kk, ct_old.astype(BF16)], axis=0), qt)
            s_t = (kc[:blk] * w_intra).astype(BF16)
            vext = jnp.concatenate([vt_ref[sl, :], ones_tail], axis=0)
            ext = _dot(vext, s_t) + w_inter * kc[blk:]
            hh = ext[:dh] / jnp.maximum(jnp.abs(ext[dh:dh + 1]), jnp.exp(-m_t))
            hh = hh * lax.rsqrt(jnp.mean(hh * hh, axis=0, keepdims=True) + EPS)
            gate = 1.0 / (1.0 + jnp.exp(-ot_ref[sl, :].astype(F32)))
            h_ref[bi, :, sl] = ((hh * gate).T * g_ref[:, sl]).astype(BF16)
            m_new = m_t[:, last:last + 1]
            b_last = b_r[:, last:last + 1]
            w_s = jnp.exp(b_last - b_r + ig_r - m_new)
            if n_valid < blk:
                w_s = jnp.where(lane1 < n_valid, w_s, 0.0)
            decay = jnp.exp(b_last + m_prev - m_new)
            vtw = (vext.astype(F32) * w_s).astype(BF16)
            ct_sc[bi, hd] = decay * ct_old + _dot(vtw, kk)
            m_sc[s_idx:s_idx + 1, :] = jnp.broadcast_to(m_new, (1, m_sc.shape[1]))

    @pl.when(j == pl.num_programs(1) - 1)
    def _store_state():
        m_out_ref[...] = m_sc[...]
        for bi in range(nbs):
            for hd in range(N_HEADS_B):
                c_out_ref[bi, hd] = ct_sc[bi, hd, :dh, :].T
                n_out_ref[bi, hd] = ct_sc[bi, hd, dh:dh + 1, :]


def _mlstm(k, ft, gcb, gcx, layer, g, c0, n0, m0, blk, n_valid):
    nb, seq, _ = k.shape
    nblk = seq // blk
    nbs = math.gcd(nb, MLSTM_BATCHES)
    dh = HEAD_DIM_B
    tok = lambda w: pl.BlockSpec((nbs, blk, w), lambda bg, j: (bg, j, 0))
    col_blk = lambda k_: (lambda bg, j: (bg * nbs + k_) * nblk + j)
    ft_specs = [pl.BlockSpec((WIDTH_B, blk), lambda bg, j, r=r, f=col_blk(k_): (r, f(bg, j)))
                for k_ in range(nbs) for r in (1, 2, 3)]
    gcb_specs = [pl.BlockSpec((2 * N_HEADS_B, blk), lambda bg, j, f=col_blk(k_): (0, f(bg, j))) for k_ in range(nbs)]
    cspec = pl.BlockSpec((nbs, N_HEADS_B, dh, dh), lambda bg, j: (bg, 0, 0, 0))
    nspec = pl.BlockSpec((nbs, N_HEADS_B, 1, dh), lambda bg, j: (bg, 0, 0, 0))
    mspec = pl.BlockSpec((nbs * N_HEADS_B, LANES), lambda bg, j: (bg, 0))
    return pl.pallas_call(
        functools.partial(_mlstm_body, n_valid=n_valid, nbs=nbs), grid=(nb // nbs, nblk),
        in_specs=[tok(WIDTH_B), tok(LANES)] + ft_specs + gcb_specs + [_layer_spec(g, layer), cspec, nspec, mspec],
        out_specs=[tok(WIDTH_B), cspec, nspec, mspec],
        out_shape=[jax.ShapeDtypeStruct(k.shape, BF16), jax.ShapeDtypeStruct(c0.shape, F32),
                   jax.ShapeDtypeStruct(n0.shape, F32), jax.ShapeDtypeStruct(m0.shape, F32)],
        scratch_shapes=[pltpu.VMEM((nbs, N_HEADS_B, 2 * dh, dh), F32), pltpu.VMEM((nbs * N_HEADS_B, LANES), F32)],
        compiler_params=_params(2), name="mlstm",
    )(k, gcx, *([ft] * (3 * nbs)), *([gcb] * nbs), g, c0, n0, m0)


def _toeplitz_bias(gext_row, rows):
    gx = jnp.broadcast_to(gext_row, (rows, BIAS_W))
    return pltpu.roll(gx, 0, 1, stride=1, stride_axis=0)


def _attn_prompt_body(q_ref, kp_ref, kc_ref, vp_ref, vc_ref, gext_ref, gatt_ref, o_ref, bias_sc, o_sc):
    b = pl.program_id(0)
    i = pl.program_id(1)
    n_pairs = N_HEADS_A // 2

    @pl.when((b == 0) & (i == 0))
    def _init_bias():
        r = lax.broadcasted_iota(jnp.int32, (ATT_GROUP, BIAS_W), 0)
        c = lax.broadcasted_iota(jnp.int32, (ATT_GROUP, BIAS_W), 1)
        first = r - (r & (CHUNK - 1))
        ok = (c >= first) & (c < first + ATT_WINDOW + CHUNK)
        for hd in range(N_HEADS_A):
            t = _toeplitz_bias(gext_ref[hd:hd + 1, :], ATT_GROUP) * LOG2E
            bias_sc[hd // 2, (hd % 2) * ATT_GROUP:(hd % 2 + 1) * ATT_GROUP, :] = jnp.where(ok, t, NEG)[:, :ATT_BAND]

    lane = lax.broadcasted_iota(jnp.int32, (ATT_GROUP, PAIR_W), 1)
    low = lane < HEAD_DIM_A

    def step(has_prev):
        for g in range(ATT_ROWS // ATT_GROUP):
            r0 = g * ATT_GROUP
            n_prev = ATT_WINDOW - r0 if has_prev else 0
            n_cur = r0 + ATT_GROUP
            for p in range(n_pairs):
                ps = slice(p * PAIR_W, (p + 1) * PAIR_W)
                qp = q_ref[r0:r0 + ATT_GROUP, ps]
                zero = jnp.zeros_like(qp)
                qq = jnp.concatenate([jnp.where(low, qp, zero), jnp.where(low, zero, qp)], axis=0)
                s_cur = _dot(qq, kc_ref[ps, :n_cur])
                bias = bias_sc[p]
                if has_prev:
                    s = jnp.concatenate([_dot(qq, kp_ref[ps, r0:]), s_cur], axis=1) + bias
                else:
                    s = s_cur + bias[:, ATT_BAND - n_cur:]
                e = jnp.exp2(s - jnp.max(s, axis=-1, keepdims=True))
                l = jnp.sum(e, axis=-1, keepdims=True)
                eb = e.astype(BF16)
                o = _dot(eb[:, n_prev:], vc_ref[:n_cur, ps])
                if has_prev:
                    o = o + _dot(eb[:, :n_prev], vp_ref[r0:, ps])
                o = o / l
                o_sc[:, ps] = jnp.where(low, o[:ATT_GROUP], o[ATT_GROUP:])
            o_ref[r0:r0 + ATT_GROUP, :] = (_rms(o_sc[...]) * gatt_ref[...]).astype(BF16)

    @pl.when(i == 0)
    def _first_block():
        step(False)

    @pl.when(i > 0)
    def _later_block():
        step(True)


def _attn_prompt(qa, kt, va, layer, gext, gatt, nb):
    t = qa.shape[0]
    nblk = t // (nb * ATT_ROWS)
    cur = pl.BlockSpec((ATT_ROWS, WIDTH_A), lambda b, i: (b * nblk + i, 0))
    prev = pl.BlockSpec((ATT_ROWS, WIDTH_A), lambda b, i: (b * nblk + jnp.maximum(i - 1, 0), 0))
    cur_t = pl.BlockSpec((WIDTH_A, ATT_ROWS), lambda b, i: (0, b * nblk + i))
    prev_t = pl.BlockSpec((WIDTH_A, ATT_ROWS), lambda b, i: (0, b * nblk + jnp.maximum(i - 1, 0)))
    return pl.pallas_call(
        _attn_prompt_body, grid=(nb, nblk),
        in_specs=[cur, prev_t, cur_t, prev, cur, _layer_spec(gext, layer), _layer_spec(gatt, layer)],
        out_specs=cur, out_shape=jax.ShapeDtypeStruct((t, WIDTH_A), BF16),
        scratch_shapes=[pltpu.VMEM((N_HEADS_A // 2, 2 * ATT_GROUP, ATT_BAND), F32),
                        pltpu.VMEM((ATT_GROUP, WIDTH_A), F32)],
        compiler_params=_params(2), name="attn_prompt",
    )(qa, kt, kt, va, va, gext, gatt)


def _attn_step_body(q_ref, kn_ref, vn_ref, ck_ref, cv_ref, gext_ref, gatt_ref, o_ref, o_sc):
    s_new = q_ref.shape[0]
    l_cache = ck_ref.shape[2]
    for hd in range(N_HEADS_A):
        sl = slice(hd * HEAD_DIM_A, (hd + 1) * HEAD_DIM_A)
        bias = _toeplitz_bias(gext_ref[hd:hd + 1, :], s_new) * LOG2E
        q = q_ref[:, sl]
        s1 = _dot(q, ck_ref[hd].astype(BF16)) + bias[:, :l_cache]
        s2 = _dot(q, kn_ref[sl, :]) + bias[:, l_cache:l_cache + s_new]
        m = jnp.maximum(jnp.max(s1, axis=-1, keepdims=True), jnp.max(s2, axis=-1, keepdims=True))
        p1 = jnp.exp2(s1 - m)
        p2 = jnp.exp2(s2 - m)
        l = jnp.sum(p1, axis=-1, keepdims=True) + jnp.sum(p2, axis=-1, keepdims=True)
        o = _dot_nt(p1.astype(BF16), cv_ref[hd].astype(BF16)) + _dot(p2.astype(BF16), vn_ref[:, sl])
        o_sc[:, sl] = o / l
    o_ref[...] = (_rms(o_sc[...]) * gatt_ref[...]).astype(BF16)


def _attn_step(qa, kt_new, va, ck_t, cv_t, layer, gext, gatt):
    _, nb, nh, dh, l_cache = ck_t.shape
    s_new = qa.shape[0] // nb
    tok = pl.BlockSpec((s_new, WIDTH_A), lambda b: (b, 0))
    cache = pl.BlockSpec((None, None, nh, dh, l_cache), lambda b: (layer, b, 0, 0, 0))
    return pl.pallas_call(
        _attn_step_body, grid=(nb,),
        in_specs=[tok, pl.BlockSpec((None, WIDTH_A, s_new), lambda b: (b, 0, 0)), tok, cache, cache,
                  _layer_spec(gext, layer), _layer_spec(gatt, layer)],
        out_specs=tok, out_shape=jax.ShapeDtypeStruct(qa.shape, BF16),
        scratch_shapes=[pltpu.VMEM((s_new, WIDTH_A), F32)],
        compiler_params=_params(1), name="attn_step",
    )(qa, kt_new, va, ck_t, cv_t, gext, gatt)


GELU_C = math.sqrt(2.0 / math.pi)
GELU_A = 0.044715


def _gelu_gate(x, up):
    e = jnp.exp2(x * ((-2.0 * GELU_C * GELU_A * LOG2E) * (x * x) + (-2.0 * GELU_C * LOG2E)))
    return (x * up) / (1.0 + e)


def _shift_rows(u, prev_tile, k):
    tm, w = u.shape
    u3 = jnp.concatenate([prev_tile[None], u.reshape(tm // SUBLANES, SUBLANES, w)], axis=0)
    r = pltpu.roll(u3, k, 1)
    sub = lax.broadcasted_iota(jnp.int32, (1, SUBLANES, w), 1)
    return jnp.where(sub < k, r[:-1], r[1:]).reshape(tm, w)


def _ffn_body(x_ref, att_ref, ml_ref, ng_ref, wo_ref, wu_ref, wc_ref, bc_ref, wd_ref, init_ref,
              y_ref, tail_ref, carry_sc, a_sc, *, seq_rows):
    tm = x_ref.shape[0]
    d_ff = wd_ref.shape[0]
    multi_seq = seq_rows < tm

    if not multi_seq:
        @pl.when(pl.program_id(0) % (seq_rows // tm) == 0)
        def _seq_start():
            carry_sc[:SUBLANES - (CONV_W - 1), :] = jnp.zeros((SUBLANES - (CONV_W - 1), carry_sc.shape[1]), F32)
            carry_sc[SUBLANES - (CONV_W - 1):, :] = init_ref[...]

    mix = _dot(att_ref[...], wo_ref[:WIDTH_A, :]) + _dot(ml_ref[...], wo_ref[WIDTH_A:, :])
    x1 = x_ref[...] + _rms(mix) * ng_ref[1:2, :]
    h2 = (_rms(x1) * ng_ref[2:3, :]).astype(BF16)
    if multi_seq:
        pos = lax.broadcasted_iota(jnp.int32, (tm, FFN_COLS), 0) & (seq_rows - 1)

    def conv(cols):
        u = _dot(h2, wu_ref[:, cols])
        if multi_seq:
            fix = init_ref[:, cols]
            um2 = jnp.where(pos < 2, fix, pltpu.roll(u, 2, 0))
            um1 = jnp.where(pos < 1, pltpu.roll(fix, tm - 1, 0), pltpu.roll(u, 1, 0))
            tail_ref[:, cols] = u
        else:
            prev = carry_sc[:, cols]
            um2 = _shift_rows(u, prev, 2)
            um1 = _shift_rows(u, prev, 1)
            carry_sc[:, cols] = u[tm - SUBLANES:, :]
            tail_ref[:, cols] = u[tm - (CONV_W - 1):, :]
        return bc_ref[:, cols] + um2 * wc_ref[0:1, cols] + um1 * wc_ref[1:2, cols] + u * wc_ref[2:3, cols]

    for c in range(d_ff // FFN_COLS):
        gate = conv(slice(c * FFN_COLS, (c + 1) * FFN_COLS))
        up = conv(slice(d_ff + c * FFN_COLS, d_ff + (c + 1) * FFN_COLS))
        a_sc[:, c * FFN_COLS:(c + 1) * FFN_COLS] = _gelu_gate(gate, up).astype(BF16)
    f = _dot(a_sc[...], wd_ref[...])
    y_ref[...] = x1 + _rms(f) * ng_ref[3:4, :]


def _ffn(x2d, att, ml, layer, ng, wo, wu, wc, bc, wd, init, seq_rows):
    t, d = x2d.shape
    d_ff = wd.shape[1]
    tm = min(FFN_ROWS, t)
    assert t % tm == 0 and d_ff % FFN_COLS == 0
    multi_seq = seq_rows < tm
    if multi_seq:
        assert tm % seq_rows == 0 and t == tm and seq_rows & (seq_rows - 1) == 0
        init_spec = pl.BlockSpec(init.shape, lambda i: (0, 0))
        tail_spec = pl.BlockSpec((tm, 2 * d_ff), lambda i: (0, 0))
        tail_shape = jax.ShapeDtypeStruct((tm, 2 * d_ff), F32)
    else:
        assert seq_rows % tm == 0
        per_seq = seq_rows // tm
        init_spec = pl.BlockSpec((None, CONV_W - 1, 2 * d_ff), lambda i: (i // per_seq, 0, 0))
        tail_spec = init_spec
        tail_shape = jax.ShapeDtypeStruct(init.shape, F32)
    row_spec = lambda w: pl.BlockSpec((tm, w), lambda i: (i, 0))
    return pl.pallas_call(
        functools.partial(_ffn_body, seq_rows=seq_rows), grid=(t // tm,),
        in_specs=([row_spec(d), row_spec(WIDTH_A), row_spec(WIDTH_B)]
                  + [_layer_spec(a, layer) for a in (ng, wo, wu, wc, bc, wd)] + [init_spec]),
        out_specs=[row_spec(d), tail_spec],
        out_shape=[jax.ShapeDtypeStruct((t, d), F32), tail_shape],
        scratch_shapes=[pltpu.VMEM((SUBLANES, 2 * d_ff), F32), pltpu.VMEM((tm, d_ff), BF16)],
        compiler_params=_params(1), name="ffn",
    )(x2d, att, ml, ng, wo, wu, wc, bc, wd, init)


def _bias_rows(rel_table_l):
    n_far = ATT_WINDOW - REL_CLIP + 1
    far = jnp.broadcast_to(rel_table_l[:, 2 * REL_CLIP:], (N_HEADS_A, n_far))
    lo = ATT_WINDOW + REL_CLIP - (ATT_BAND - 1)
    near = rel_table_l[:, lo:2 * REL_CLIP][:, ::-1]
    wrap = jnp.broadcast_to(rel_table_l[:, 2 * REL_CLIP:], (N_HEADS_A, BIAS_W - ATT_BAND))
    return jnp.concatenate([far, near, wrap], axis=1)


def _pad_seq(a, seq_pad):
    return a if a.shape[1] == seq_pad else jnp.pad(a, ((0, 0), (0, seq_pad - a.shape[1]), (0, 0)))


def _pad_cols(a, nb, seq, seq_pad):
    if seq == seq_pad:
        return a
    return jnp.pad(a.reshape(-1, nb, seq), ((0, 0), (0, 0), (0, seq_pad - seq))).reshape(a.shape[0], -1)


def _layer(x, k_cache, v_cache, c0, n0, m0, conv_buf, w):
    nb, seq, d = x.shape
    x2d = x.reshape(nb * seq, d)
    blk = MLSTM_BLOCK if seq % MLSTM_BLOCK == 0 else LANES
    seg = min(blk, seq)
    l = w["layer"]
    qa, va, kb, ft, gcb, gcx = _inproj(x2d, l, w["norm_g"], w["w_main"], w["w_ft"], w["gate_bias"], seg)

    keep = min(ATT_WINDOW, seq) if k_cache is None else seq
    k_tail = jnp.stack([ft[:WIDTH_A, (b + 1) * seq - keep:(b + 1) * seq] for b in range(nb)])
    if k_cache is None:
        att = _attn_prompt(qa, ft, va, l, w["gext"], w["g_att"], nb)
    else:
        att = _attn_step(qa, k_tail, va, k_cache, v_cache, l, w["gext"], w["g_att"])
    new_k = k_tail.reshape(nb, N_HEADS_A, HEAD_DIM_A, keep).astype(F32).swapaxes(-1, -2)
    new_v = va.reshape(nb, seq, WIDTH_A)[:, seq - keep:].reshape(nb, keep, N_HEADS_A, HEAD_DIM_A)
    new_v = new_v.transpose(0, 2, 1, 3).astype(F32)

    seq_pad = -(-seq // blk) * blk
    m0b = jnp.broadcast_to(m0.reshape(-1, 1), (nb * N_HEADS_B, LANES))
    r3 = lambda a: _pad_seq(a.reshape(nb, seq, -1), seq_pad)
    hb, c1, n1, m_b = _mlstm(r3(kb), _pad_cols(ft, nb, seq, seq_pad), _pad_cols(gcb, nb, seq, seq_pad), r3(gcx),
                             l, w["g_mlstm"], c0, n0[:, :, None, :], m0b, blk, seq - (seq_pad - blk))
    hb = hb[:, :seq].reshape(nb * seq, WIDTH_B)
    n1, m1 = n1[:, :, 0, :], m_b[:, 0].reshape(nb, N_HEADS_B)

    tm = min(FFN_ROWS, nb * seq)
    if seq < tm:
        init = jnp.pad(conv_buf, ((0, 0), (0, seq - (CONV_W - 1)), (0, 0))).reshape(nb * seq, -1)
        y, u = _ffn(x2d, att, hb, l, w["norm_g"], w["w_out"], w["w_up"], w["w_conv"], w["b_conv"],
                    w["w_down"], init, seq)
        new_buf = u.reshape(nb, seq, -1)[:, seq - (CONV_W - 1):]
    else:
        y, new_buf = _ffn(x2d, att, hb, l, w["norm_g"], w["w_out"], w["w_up"], w["w_conv"], w["b_conv"],
                          w["w_down"], conv_buf, seq)
    return y.reshape(nb, seq, d), new_k, new_v, c1, n1, m1, new_buf


def _stacked_weights(norm_g, w_in, b_i, b_f, rel_table, g_att, g_mlstm, w_out, w_up, w_conv, b_conv, w_down):
    wa, wb = WIDTH_A, WIDTH_B
    assert wa == wb
    col = lambda k: slice(k * wa, (k + 1) * wa)
    wi = w_in.astype(BF16)
    w_main = jnp.concatenate([wi[..., col(k)] for k in (0, 2, 4)], axis=-1)
    wg = w_in[..., 7 * wa:]
    wg_hi = wg.astype(BF16)
    wg_lo = (wg - wg_hi.astype(F32)).astype(BF16)
    w_ft = jnp.concatenate([wi[..., col(k)] for k in (1, 3, 5, 6)] + [wg_hi, wg_lo], axis=-1).swapaxes(1, 2)
    return dict(
        norm_g=norm_g, w_main=w_main, w_ft=w_ft, gate_bias=jnp.concatenate([b_i, b_f], axis=-1)[..., None],
        gext=jax.vmap(_bias_rows)(rel_table), g_att=g_att[:, None, :], g_mlstm=g_mlstm[:, None, :],
        w_out=w_out.astype(BF16), w_up=w_up.astype(BF16), w_conv=w_conv, b_conv=b_conv[:, None, :],
        w_down=w_down.astype(BF16))


def kernel(x_prompt, x_sample, cache_k_att, cache_v_att, state_mlstm_c, state_mlstm_n, state_mlstm_m,
           cache_ffn_conv, norm_g, w_in, b_i, b_f, rel_table, g_att, g_mlstm, w_out, w_up, w_conv,
           b_conv, w_down):
    depth = w_in.shape[0]
    bp = x_prompt.shape[0]
    d_ff2 = w_up.shape[2]
    dh = HEAD_DIM_B
    c_zero = jnp.zeros((bp, N_HEADS_B, dh, dh), F32)
    n_zero = jnp.zeros((bp, N_HEADS_B, dh), F32)
    m_zero = jnp.zeros((bp, N_HEADS_B), F32)
    buf_zero = jnp.zeros((bp, CONV_W - 1, d_ff2), F32)
    xp, xs = x_prompt, x_sample
    ck_t = cache_k_att.swapaxes(-1, -2)
    cv_t = cache_v_att.swapaxes(-1, -2)
    outs_p, outs_s = [], []
    stacks = _stacked_weights(norm_g, w_in, b_i, b_f, rel_table, g_att, g_mlstm, w_out, w_up, w_conv, b_conv,
                              w_down)
    for l in range(depth):
        w = dict(stacks, layer=l)
        xp, *st_p = _layer(xp, None, None, c_zero, n_zero, m_zero, buf_zero, w)
        xs, *st_s = _layer(xs, ck_t, cv_t, state_mlstm_c[l], state_mlstm_n[l],
                           state_mlstm_m[l], cache_ffn_conv[l], w)
        outs_p.append(st_p)
        outs_s.append(st_s)
    stack = lambda outs, i: jnp.stack([o[i] for o in outs])
    return (xp, xs) + tuple(stack(outs_p, i) for i in range(6)) + tuple(stack(outs_s, i) for i in range(6))
```

```python
import functools
import math

import jax
import jax.numpy as jnp
from jax import lax
from jax.experimental import pallas as pl
from jax.experimental.pallas import tpu as pltpu

F32 = jnp.float32
BF16 = jnp.bfloat16

CHUNK = 64
LEFT_CHUNKS = 8
ATT_WINDOW = LEFT_CHUNKS * CHUNK
REL_CLIP = 256
N_HEADS_A = 8
HEAD_DIM_A = 64
WIDTH_A = N_HEADS_A * HEAD_DIM_A
N_HEADS_B = 4
HEAD_DIM_B = 128
WIDTH_B = N_HEADS_B * HEAD_DIM_B
CONV_W = 3
EPS = 1e-6
NEG = -1e30
LOG2E = math.log2(math.e)

LANES = 128
SUBLANES = 8
VMEM_LIMIT_BYTES = 56 * 1024 * 1024

INPROJ_ROWS = 512
ATT_ROWS = ATT_WINDOW
ATT_GROUP = 4 * CHUNK
ATT_BAND = ATT_GROUP + ATT_WINDOW
BIAS_W = 1024
PAIR_W = 2 * HEAD_DIM_A
ATT_STEP_BATCHES = 2
MLSTM_BLOCK = 256
MLSTM_BATCHES = 2
MLSTM_EXT_PAD = 16
FFN_ROWS = 512
FFN_COLS = 256
FFN_SLAB = 64


def _params(n_axes):
    return pltpu.CompilerParams(dimension_semantics=("arbitrary",) * n_axes,
                                vmem_limit_bytes=VMEM_LIMIT_BYTES)


def _layer_spec(stacked, layer):
    rest = (0,) * (stacked.ndim - 1)
    return pl.BlockSpec((None,) + stacked.shape[1:], lambda *_: (layer,) + rest, pipeline_mode=pl.Buffered(1))


def _rms(x):
    return x * lax.rsqrt(jnp.mean(x * x, axis=-1, keepdims=True) + EPS)


def _dot(a, b):
    return jnp.dot(a, b, preferred_element_type=F32)


def _dot_nt(a, b):
    return lax.dot_general(a, b, (((1,), (1,)), ((), ())), preferred_element_type=F32)


def _inproj_body(x_ref, g_ref, wm_ref, wft_ref, gb_ref,
                 qa_ref, va_ref, kb_ref, ft_ref, gcb_ref, gcx_ref, *, seg):
    n_feat = ft_ref.shape[0]
    tm = x_ref.shape[0]
    hb = (_rms(x_ref[...]) * g_ref[0:1, :]).astype(BF16)
    zf = _dot_nt(wft_ref[...], hb)
    ft_ref[...] = zf[:n_feat].astype(BF16)
    pre = zf[n_feat:n_feat + 2 * N_HEADS_B] + zf[n_feat + 2 * N_HEADS_B:] + gb_ref[...]
    lf = jnp.minimum(pre, 0.0) - jnp.log(1.0 + jnp.exp(-jnp.abs(pre)))
    pos = lax.broadcasted_iota(jnp.int32, pre.shape, 1) & (seg - 1)
    k = 1
    while k < seg:
        lf = lf + jnp.where(pos >= k, pltpu.roll(lf, k, 1), 0.0)
        k *= 2
    row = lax.broadcasted_iota(jnp.int32, pre.shape, 0)
    gcb = jnp.where(row < N_HEADS_B, pre, lf)
    gcb_ref[...] = gcb
    c = gcb[:N_HEADS_B] - gcb[N_HEADS_B:]
    c1 = c.astype(BF16).astype(F32)
    c2 = (c - c1).astype(BF16).astype(F32)
    c3 = c - c1 - c2
    r8 = lax.broadcasted_iota(jnp.int32, (SUBLANES, tm), 0)
    tiles = [jnp.where(r8 == 0, c1[h:h + 1], jnp.where(r8 == 1, c2[h:h + 1], jnp.where(r8 == 2, c3[h:h + 1], 0.0)))
             for h in range(N_HEADS_B)]
    tiles.append(jnp.zeros((LANES - N_HEADS_B * SUBLANES, tm), F32))
    gcx_ref[...] = jnp.concatenate(tiles, axis=0).T.astype(BF16)
    for c_idx, (o_ref, scale) in enumerate(((qa_ref, HEAD_DIM_A ** -0.5 * LOG2E), (va_ref, None),
                                            (kb_ref, HEAD_DIM_B ** -0.5))):
        z = _dot(hb, wm_ref[:, c_idx * WIDTH_A:(c_idx + 1) * WIDTH_A])
        if scale is not None:
            z = z * scale
        o_ref[...] = z.astype(BF16)


def _inproj(x2d, layer, ng, wm, wft, gb, seg):
    t, d = x2d.shape
    tm = min(INPROJ_ROWS, t)
    assert t % tm == 0 and tm % seg == 0 and seg & (seg - 1) == 0
    n_feat = wft.shape[1] - 4 * N_HEADS_B
    row_spec = lambda w: pl.BlockSpec((tm, w), lambda i: (i, 0))
    col_spec = lambda r: pl.BlockSpec((r, tm), lambda i: (0, i))
    out_shape = [jax.ShapeDtypeStruct((t, WIDTH_A), BF16)] * 3 + [
        jax.ShapeDtypeStruct((n_feat, t), BF16), jax.ShapeDtypeStruct((2 * N_HEADS_B, t), F32),
        jax.ShapeDtypeStruct((t, LANES), BF16)]
    out_specs = [row_spec(WIDTH_A)] * 3 + [col_spec(n_feat), col_spec(2 * N_HEADS_B), row_spec(LANES)]
    return pl.pallas_call(
        functools.partial(_inproj_body, seg=seg), grid=(t // tm,),
        in_specs=[row_spec(d)] + [_layer_spec(a, layer) for a in (ng, wm, wft, gb)],
        out_specs=out_specs, out_shape=out_shape, compiler_params=_params(1), name="inproj",
    )(x2d, ng, wm, wft, gb)


def _mlstm_body(*refs, n_valid, nbs):
    k_ref, gcx_ref = refs[:2]
    ft_refs = refs[2:2 + 3 * nbs]
    gcb_refs = refs[2 + 3 * nbs:2 + 4 * nbs]
    (g_ref, c0_ref, n0_ref, m0_ref, h_ref, c_out_ref, n_out_ref, m_out_ref, ct_sc, m_sc) = refs[2 + 4 * nbs:]
    j = pl.program_id(1)
    blk = k_ref.shape[1]
    dh = HEAD_DIM_B
    ext_rows = ct_sc.shape[2]

    @pl.when(j == 0)
    def _load_state():
        m_sc[...] = m0_ref[...]
        for bi in range(nbs):
            for hd in range(N_HEADS_B):
                ct_sc[bi, hd, :dh, :] = c0_ref[bi, hd].T
                ct_sc[bi, hd, dh:, :] = jnp.concatenate([n0_ref[bi, hd], jnp.zeros((ext_rows - dh - 1, dh), F32)], axis=0)

    row = lax.broadcasted_iota(jnp.int32, (blk, blk), 0)
    col = lax.broadcasted_iota(jnp.int32, (blk, blk), 1)
    valid = row <= col
    if n_valid < blk:
        valid = valid & (row < n_valid)
    lane1 = lax.broadcasted_iota(jnp.int32, (1, blk), 1)
    r128 = lax.broadcasted_iota(jnp.int32, (LANES, blk), 0)
    r_tail = lax.broadcasted_iota(jnp.int32, (ext_rows - dh, blk), 0)
    ones_tail = jnp.where(r_tail == 0, 1.0, 0.0).astype(BF16)
    last = n_valid - 1
    for bi in range(nbs):
        qt_ref, vt_ref, ot_ref = ft_refs[3 * bi:3 * bi + 3]
        for hd in range(N_HEADS_B):
            s_idx = bi * N_HEADS_B + hd
            sl = slice(hd * dh, (hd + 1) * dh)
            qt = qt_ref[sl, :]
            kk = k_ref[bi, :, sl]
            ig_r = gcb_refs[bi][hd:hd + 1, :]
            b_r = gcb_refs[bi][N_HEADS_B + hd:N_HEADS_B + hd + 1, :]
            m_prev = m_sc[s_idx:s_idx + 1, 0:1]
            pick = jnp.where((r128 >= hd * SUBLANES) & (r128 < hd * SUBLANES + 3), 1.0, 0.0).astype(BF16)
            dmat = jnp.where(valid, _dot(gcx_ref[bi], pick) + b_r, NEG)
            inter = b_r + m_prev
            m_t = jnp.maximum(inter, jnp.max(dmat, axis=0, keepdims=True))
            w_intra = jnp.exp(dmat - m_t)
            w_inter = jnp.exp(inter - m_t)
            ct_old = ct_sc[bi, hd]
            kc = _dot(jnp.concatenate([kk, ct_old.astype(BF16)], axis=0), qt)
            s_t = (kc[:blk] * w_intra).astype(BF16)
            vext = jnp.concatenate([vt_ref[sl, :], ones_tail], axis=0)
            ext = _dot(vext, s_t) + w_inter * kc[blk:]
            hh = ext[:dh] / jnp.maximum(jnp.abs(ext[dh:dh + 1]), jnp.exp(-m_t))
            hh = hh * lax.rsqrt(jnp.mean(hh * hh, axis=0, keepdims=True) + EPS)
            gate = 1.0 / (1.0 + jnp.exp(-ot_ref[sl, :].astype(F32)))
            h_ref[bi, :, sl] = ((hh * gate).T * g_ref[:, sl]).astype(BF16)
            m_new = m_t[:, last:last + 1]
            b_last = b_r[:, last:last + 1]
            w_s = jnp.exp(b_last - b_r + ig_r - m_new)
            if n_valid < blk:
                w_s = jnp.where(lane1 < n_valid, w_s, 0.0)
            decay = jnp.exp(b_last + m_prev - m_new)
            vtw = (vext.astype(F32) * w_s).astype(BF16)
            ct_sc[bi, hd] = decay * ct_old + _dot(vtw, kk)
            m_sc[s_idx:s_idx + 1, :] = jnp.broadcast_to(m_new, (1, m_sc.shape[1]))

    @pl.when(j == pl.num_programs(1) - 1)
    def _store_state():
        m_out_ref[...] = m_sc[...]
        for bi in range(nbs):
            for hd in range(N_HEADS_B):
                c_out_ref[bi, hd] = ct_sc[bi, hd, :dh, :].T
                n_out_ref[bi, hd] = ct_sc[bi, hd, dh:dh + 1, :]


def _mlstm(k, ft, gcb, gcx, layer, g, c0, n0, m0, blk, n_valid):
    nb, seq, _ = k.shape
    nblk = seq // blk
    nbs = math.gcd(nb, MLSTM_BATCHES)
    dh = HEAD_DIM_B
    tok = lambda w: pl.BlockSpec((nbs, blk, w), lambda bg, j: (bg, j, 0))
    col_blk = lambda k_: (lambda bg, j: (bg * nbs + k_) * nblk + j)
    ft_specs = [pl.BlockSpec((WIDTH_B, blk), lambda bg, j, r=r, f=col_blk(k_): (r, f(bg, j)))
                for k_ in range(nbs) for r in (1, 2, 3)]
    gcb_specs = [pl.BlockSpec((2 * N_HEADS_B, blk), lambda bg, j, f=col_blk(k_): (0, f(bg, j))) for k_ in range(nbs)]
    cspec = pl.BlockSpec((nbs, N_HEADS_B, dh, dh), lambda bg, j: (bg, 0, 0, 0))
    nspec = pl.BlockSpec((nbs, N_HEADS_B, 1, dh), lambda bg, j: (bg, 0, 0, 0))
    mspec = pl.BlockSpec((nbs * N_HEADS_B, LANES), lambda bg, j: (bg, 0))
    return pl.pallas_call(
        functools.partial(_mlstm_body, n_valid=n_valid, nbs=nbs), grid=(nb // nbs, nblk),
        in_specs=[tok(WIDTH_B), tok(LANES)] + ft_specs + gcb_specs + [_layer_spec(g, layer), cspec, nspec, mspec],
        out_specs=[tok(WIDTH_B), cspec, nspec, mspec],
        out_shape=[jax.ShapeDtypeStruct(k.shape, BF16), jax.ShapeDtypeStruct(c0.shape, F32),
                   jax.ShapeDtypeStruct(n0.shape, F32), jax.ShapeDtypeStruct(m0.shape, F32)],
        scratch_shapes=[pltpu.VMEM((nbs, N_HEADS_B, dh + MLSTM_EXT_PAD, dh), F32),
                        pltpu.VMEM((nbs * N_HEADS_B, LANES), F32)],
        compiler_params=_params(2), name="mlstm",
    )(k, gcx, *([ft] * (3 * nbs)), *([gcb] * nbs), g, c0, n0, m0)


def _toeplitz_bias(gext_row, rows):
    gx = jnp.broadcast_to(gext_row, (rows, BIAS_W))
    return pltpu.roll(gx, 0, 1, stride=1, stride_axis=0)


def _attn_prompt_body(q_ref, kp_ref, kc_ref, vp_ref, vc_ref, gext_ref, gatt_ref, o_ref, bias_sc, o_sc):
    b = pl.program_id(0)
    i = pl.program_id(1)
    n_pairs = N_HEADS_A // 2

    @pl.when((b == 0) & (i == 0))
    def _init_bias():
        r = lax.broadcasted_iota(jnp.int32, (ATT_GROUP, BIAS_W), 0)
        c = lax.broadcasted_iota(jnp.int32, (ATT_GROUP, BIAS_W), 1)
        first = r - (r & (CHUNK - 1))
        ok = (c >= first) & (c < first + ATT_WINDOW + CHUNK)
        for hd in range(N_HEADS_A):
            t = _toeplitz_bias(gext_ref[hd:hd + 1, :], ATT_GROUP) * LOG2E
            bias_sc[hd // 2, (hd % 2) * ATT_GROUP:(hd % 2 + 1) * ATT_GROUP, :] = jnp.where(ok, t, NEG)[:, :ATT_BAND]

    lane = lax.broadcasted_iota(jnp.int32, (ATT_GROUP, PAIR_W), 1)
    low = lane < HEAD_DIM_A

    def step(has_prev):
        for g in range(ATT_ROWS // ATT_GROUP):
            r0 = g * ATT_GROUP
            n_prev = ATT_WINDOW - r0 if has_prev else 0
            n_cur = r0 + ATT_GROUP
            for p in range(n_pairs):
                ps = slice(p * PAIR_W, (p + 1) * PAIR_W)
                qp = q_ref[r0:r0 + ATT_GROUP, ps]
                zero = jnp.zeros_like(qp)
                qq = jnp.concatenate([jnp.where(low, qp, zero), jnp.where(low, zero, qp)], axis=0)
                s_cur = _dot(qq, kc_ref[ps, :n_cur])
                bias = bias_sc[p]
                if has_prev:
                    s = jnp.concatenate([_dot(qq, kp_ref[ps, r0:]), s_cur], axis=1) + bias
                else:
                    s = s_cur + bias[:, ATT_BAND - n_cur:]
                e = jnp.exp2(s - jnp.max(s, axis=-1, keepdims=True))
                l = jnp.sum(e, axis=-1, keepdims=True)
                eb = e.astype(BF16)
                o = _dot(eb[:, n_prev:], vc_ref[:n_cur, ps])
                if has_prev:
                    o = o + _dot(eb[:, :n_prev], vp_ref[r0:, ps])
                o = o / l
                o_sc[:, ps] = jnp.where(low, o[:ATT_GROUP], o[ATT_GROUP:])
            o_ref[r0:r0 + ATT_GROUP, :] = (_rms(o_sc[...]) * gatt_ref[...]).astype(BF16)

    @pl.when(i == 0)
    def _first_block():
        step(False)

    @pl.when(i > 0)
    def _later_block():
        step(True)


def _attn_prompt(qa, kt, va, layer, gext, gatt, nb):
    t = qa.shape[0]
    nblk = t // (nb * ATT_ROWS)
    cur = pl.BlockSpec((ATT_ROWS, WIDTH_A), lambda b, i: (b * nblk + i, 0))
    prev = pl.BlockSpec((ATT_ROWS, WIDTH_A), lambda b, i: (b * nblk + jnp.maximum(i - 1, 0), 0))
    cur_t = pl.BlockSpec((WIDTH_A, ATT_ROWS), lambda b, i: (0, b * nblk + i))
    prev_t = pl.BlockSpec((WIDTH_A, ATT_ROWS), lambda b, i: (0, b * nblk + jnp.maximum(i - 1, 0)))
    return pl.pallas_call(
        _attn_prompt_body, grid=(nb, nblk),
        in_specs=[cur, prev_t, cur_t, prev, cur, _layer_spec(gext, layer), _layer_spec(gatt, layer)],
        out_specs=cur, out_shape=jax.ShapeDtypeStruct((t, WIDTH_A), BF16),
        scratch_shapes=[pltpu.VMEM((N_HEADS_A // 2, 2 * ATT_GROUP, ATT_BAND), F32),
                        pltpu.VMEM((ATT_GROUP, WIDTH_A), F32)],
        compiler_params=_params(2), name="attn_prompt",
    )(qa, kt, kt, va, va, gext, gatt)


def _attn_step_body(q_ref, kn_ref, vn_ref, ck_ref, cv_ref, gext_ref, gatt_ref, o_ref, o_sc):
    nbs, _, s_new = kn_ref.shape
    l_cache = ck_ref.shape[3]
    for hd in range(N_HEADS_A):
        sl = slice(hd * HEAD_DIM_A, (hd + 1) * HEAD_DIM_A)
        bias = _toeplitz_bias(gext_ref[hd:hd + 1, :], s_new) * LOG2E
        for bi in range(nbs):
            rows = slice(bi * s_new, (bi + 1) * s_new)
            q = q_ref[rows, sl]
            s1 = _dot(q, ck_ref[bi, hd].astype(BF16)) + bias[:, :l_cache]
            s2 = _dot(q, kn_ref[bi, sl, :]) + bias[:, l_cache:l_cache + s_new]
            m = jnp.maximum(jnp.max(s1, axis=-1, keepdims=True), jnp.max(s2, axis=-1, keepdims=True))
            p1 = jnp.exp2(s1 - m)
            p2 = jnp.exp2(s2 - m)
            l = jnp.sum(p1, axis=-1, keepdims=True) + jnp.sum(p2, axis=-1, keepdims=True)
            o = _dot_nt(p1.astype(BF16), cv_ref[bi, hd].astype(BF16)) + _dot(p2.astype(BF16), vn_ref[rows, sl])
            o_sc[rows, sl] = o / l
    o_ref[...] = (_rms(o_sc[...]) * gatt_ref[...]).astype(BF16)


def _attn_step(qa, kt_new, va, ck_t, cv_t, layer, gext, gatt):
    _, nb, nh, dh, l_cache = ck_t.shape
    s_new = qa.shape[0] // nb
    nbs = math.gcd(nb, ATT_STEP_BATCHES)
    tok = pl.BlockSpec((nbs * s_new, WIDTH_A), lambda b: (b, 0))
    cache = pl.BlockSpec((None, nbs, nh, dh, l_cache), lambda b: (layer, b, 0, 0, 0))
    return pl.pallas_call(
        _attn_step_body, grid=(nb // nbs,),
        in_specs=[tok, pl.BlockSpec((nbs, WIDTH_A, s_new), lambda b: (b, 0, 0)), tok, cache, cache,
                  _layer_spec(gext, layer), _layer_spec(gatt, layer)],
        out_specs=tok, out_shape=jax.ShapeDtypeStruct(qa.shape, BF16),
        scratch_shapes=[pltpu.VMEM((nbs * s_new, WIDTH_A), F32)],
        compiler_params=_params(1), name="attn_step",
    )(qa, kt_new, va, ck_t, cv_t, gext, gatt)


GELU_C = math.sqrt(2.0 / math.pi)
GELU_A = 0.044715


def _gelu_gate(x, up):
    e = jnp.exp2(x * ((-2.0 * GELU_C * GELU_A * LOG2E) * (x * x) + (-2.0 * GELU_C * LOG2E)))
    return (x * up) / (1.0 + e)


def _shift_rows(u, prev_tile, k):
    tm, w = u.shape
    u3 = jnp.concatenate([prev_tile[None], u.reshape(tm // SUBLANES, SUBLANES, w)], axis=0)
    r = pltpu.roll(u3, k, 1)
    sub = lax.broadcasted_iota(jnp.int32, (1, SUBLANES, w), 1)
    return jnp.where(sub < k, r[:-1], r[1:]).reshape(tm, w)


def _ffn_body(x_ref, att_ref, ml_ref, ng_ref, wo_ref, wu_ref, wc_ref, bc_ref, wd_ref, init_ref,
              y_ref, tail_ref, carry_sc, a_sc, *, seq_rows):
    tm = x_ref.shape[0]
    d_ff = wd_ref.shape[0]
    multi_seq = seq_rows < tm

    if not multi_seq:
        @pl.when(pl.program_id(0) % (seq_rows // tm) == 0)
        def _seq_start():
            carry_sc[:SUBLANES - (CONV_W - 1), :] = jnp.zeros((SUBLANES - (CONV_W - 1), carry_sc.shape[1]), F32)
            carry_sc[SUBLANES - (CONV_W - 1):, :] = init_ref[...]

    mix = _dot(att_ref[...], wo_ref[:WIDTH_A, :]) + _dot(ml_ref[...], wo_ref[WIDTH_A:, :])
    x1 = x_ref[...] + _rms(mix) * ng_ref[1:2, :]
    h2 = (_rms(x1) * ng_ref[2:3, :]).astype(BF16)
    if multi_seq:
        pos = lax.broadcasted_iota(jnp.int32, (tm, FFN_COLS), 0) & (seq_rows - 1)

    def up_project(cols):
        u = _dot(h2, wu_ref[:, cols])
        if multi_seq:
            tail_ref[:, cols] = u
            return u, init_ref[:, cols]
        prev = carry_sc[:, cols]
        carry_sc[:, cols] = u[tm - SUBLANES:, :]
        tail_ref[:, cols] = u[tm - (CONV_W - 1):, :]
        return u, prev

    def conv(u, prev, cols, r0, rows):
        us = u[r0:r0 + rows]
        if multi_seq:
            fix = prev[r0:r0 + rows]
            ps = pos[r0:r0 + rows]
            um2 = jnp.where(ps < 2, fix, pltpu.roll(u, 2, 0)[r0:r0 + rows])
            um1 = jnp.where(ps < 1, pltpu.roll(prev, tm - 1, 0)[r0:r0 + rows], pltpu.roll(u, 1, 0)[r0:r0 + rows])
        else:
            before = prev if r0 == 0 else u[r0 - SUBLANES:r0]
            um2 = _shift_rows(us, before, 2)
            um1 = _shift_rows(us, before, 1)
        return bc_ref[:, cols] + um2 * wc_ref[0:1, cols] + um1 * wc_ref[1:2, cols] + us * wc_ref[2:3, cols]

    slab = tm if multi_seq else min(FFN_SLAB, tm)
    for c in range(d_ff // FFN_COLS):
        gcols = slice(c * FFN_COLS, (c + 1) * FFN_COLS)
        ucols = slice(d_ff + c * FFN_COLS, d_ff + (c + 1) * FFN_COLS)
        ug, pg = up_project(gcols)
        uu, pu = up_project(ucols)
        for r0 in range(0, tm, slab):
            act = _gelu_gate(conv(ug, pg, gcols, r0, slab), conv(uu, pu, ucols, r0, slab))
            a_sc[r0:r0 + slab, gcols] = act.astype(BF16)
    f = _dot(a_sc[...], wd_ref[...])
    y_ref[...] = x1 + _rms(f) * ng_ref[3:4, :]


def _ffn(x2d, att, ml, layer, ng, wo, wu, wc, bc, wd, init, seq_rows):
    t, d = x2d.shape
    d_ff = wd.shape[1]
    tm = min(FFN_ROWS, t)
    assert t % tm == 0 and d_ff % FFN_COLS == 0
    multi_seq = seq_rows < tm
    if multi_seq:
        assert tm % seq_rows == 0 and t == tm and seq_rows & (seq_rows - 1) == 0
        init_spec = pl.BlockSpec(init.shape, lambda i: (0, 0))
        tail_spec = pl.BlockSpec((tm, 2 * d_ff), lambda i: (0, 0))
        tail_shape = jax.ShapeDtypeStruct((tm, 2 * d_ff), F32)
    else:
        assert seq_rows % tm == 0
        per_seq = seq_rows // tm
        init_spec = pl.BlockSpec((None, CONV_W - 1, 2 * d_ff), lambda i: (i // per_seq, 0, 0))
        tail_spec = init_spec
        tail_shape = jax.ShapeDtypeStruct(init.shape, F32)
    row_spec = lambda w: pl.BlockSpec((tm, w), lambda i: (i, 0))
    return pl.pallas_call(
        functools.partial(_ffn_body, seq_rows=seq_rows), grid=(t // tm,),
        in_specs=([row_spec(d), row_spec(WIDTH_A), row_spec(WIDTH_B)]
                  + [_layer_spec(a, layer) for a in (ng, wo, wu, wc, bc, wd)] + [init_spec]),
        out_specs=[row_spec(d), tail_spec],
        out_shape=[jax.ShapeDtypeStruct((t, d), F32), tail_shape],
        scratch_shapes=[pltpu.VMEM((SUBLANES, 2 * d_ff), F32), pltpu.VMEM((tm, d_ff), BF16)],
        compiler_params=_params(1), name="ffn",
    )(x2d, att, ml, ng, wo, wu, wc, bc, wd, init)


def _bias_rows(rel_table_l):
    n_far = ATT_WINDOW - REL_CLIP + 1
    far = jnp.broadcast_to(rel_table_l[:, 2 * REL_CLIP:], (N_HEADS_A, n_far))
    lo = ATT_WINDOW + REL_CLIP - (ATT_BAND - 1)
    near = rel_table_l[:, lo:2 * REL_CLIP][:, ::-1]
    wrap = jnp.broadcast_to(rel_table_l[:, 2 * REL_CLIP:], (N_HEADS_A, BIAS_W - ATT_BAND))
    return jnp.concatenate([far, near, wrap], axis=1)


def _pad_seq(a, seq_pad):
    return a if a.shape[1] == seq_pad else jnp.pad(a, ((0, 0), (0, seq_pad - a.shape[1]), (0, 0)))


def _pad_cols(a, nb, seq, seq_pad):
    if seq == seq_pad:
        return a
    return jnp.pad(a.reshape(-1, nb, seq), ((0, 0), (0, 0), (0, seq_pad - seq))).reshape(a.shape[0], -1)


def _layer(x, k_cache, v_cache, c0, n0, m0, conv_buf, w):
    nb, seq, d = x.shape
    x2d = x.reshape(nb * seq, d)
    blk = MLSTM_BLOCK if seq % MLSTM_BLOCK == 0 else LANES
    seg = min(blk, seq)
    l = w["layer"]
    qa, va, kb, ft, gcb, gcx = _inproj(x2d, l, w["norm_g"], w["w_main"], w["w_ft"], w["gate_bias"], seg)

    keep = min(ATT_WINDOW, seq) if k_cache is None else seq
    k_tail = jnp.stack([ft[:WIDTH_A, (b + 1) * seq - keep:(b + 1) * seq] for b in range(nb)])
    if k_cache is None:
        att = _attn_prompt(qa, ft, va, l, w["gext"], w["g_att"], nb)
    else:
        att = _attn_step(qa, k_tail, va, k_cache, v_cache, l, w["gext"], w["g_att"])
    new_k = k_tail.reshape(nb, N_HEADS_A, HEAD_DIM_A, keep).astype(F32).swapaxes(-1, -2)
    new_v = va.reshape(nb, seq, WIDTH_A)[:, seq - keep:].reshape(nb, keep, N_HEADS_A, HEAD_DIM_A)
    new_v = new_v.transpose(0, 2, 1, 3).astype(F32)

    seq_pad = -(-seq // blk) * blk
    m0b = jnp.broadcast_to(m0.reshape(-1, 1), (nb * N_HEADS_B, LANES))
    r3 = lambda a: _pad_seq(a.reshape(nb, seq, -1), seq_pad)
    hb, c1, n1, m_b = _mlstm(r3(kb), _pad_cols(ft, nb, seq, seq_pad), _pad_cols(gcb, nb, seq, seq_pad), r3(gcx),
                             l, w["g_mlstm"], c0, n0[:, :, None, :], m0b, blk, seq - (seq_pad - blk))
    hb = hb[:, :seq].reshape(nb * seq, WIDTH_B)
    n1, m1 = n1[:, :, 0, :], m_b[:, 0].reshape(nb, N_HEADS_B)

    tm = min(FFN_ROWS, nb * seq)
    if seq < tm:
        init = jnp.pad(conv_buf, ((0, 0), (0, seq - (CONV_W - 1)), (0, 0))).reshape(nb * seq, -1)
        y, u = _ffn(x2d, att, hb, l, w["norm_g"], w["w_out"], w["w_up"], w["w_conv"], w["b_conv"],
                    w["w_down"], init, seq)
        new_buf = u.reshape(nb, seq, -1)[:, seq - (CONV_W - 1):]
    else:
        y, new_buf = _ffn(x2d, att, hb, l, w["norm_g"], w["w_out"], w["w_up"], w["w_conv"], w["b_conv"],
                          w["w_down"], conv_buf, seq)
    return y.reshape(nb, seq, d), new_k, new_v, c1, n1, m1, new_buf


def _stacked_weights(norm_g, w_in, b_i, b_f, rel_table, g_att, g_mlstm, w_out, w_up, w_conv, b_conv, w_down):
    wa, wb = WIDTH_A, WIDTH_B
    assert wa == wb
    col = lambda k: slice(k * wa, (k + 1) * wa)
    wi = w_in.astype(BF16)
    w_main = jnp.concatenate([wi[..., col(k)] for k in (0, 2, 4)], axis=-1)
    wg = w_in[..., 7 * wa:]
    wg_hi = wg.astype(BF16)
    wg_lo = (wg - wg_hi.astype(F32)).astype(BF16)
    w_ft = jnp.concatenate([wi[..., col(k)] for k in (1, 3, 5, 6)] + [wg_hi, wg_lo], axis=-1).swapaxes(1, 2)
    return dict(
        norm_g=norm_g, w_main=w_main, w_ft=w_ft, gate_bias=jnp.concatenate([b_i, b_f], axis=-1)[..., None],
        gext=jax.vmap(_bias_rows)(rel_table), g_att=g_att[:, None, :], g_mlstm=g_mlstm[:, None, :],
        w_out=w_out.astype(BF16), w_up=w_up.astype(BF16), w_conv=w_conv, b_conv=b_conv[:, None, :],
        w_down=w_down.astype(BF16))


def kernel(x_prompt, x_sample, cache_k_att, cache_v_att, state_mlstm_c, state_mlstm_n, state_mlstm_m,
           cache_ffn_conv, norm_g, w_in, b_i, b_f, rel_table, g_att, g_mlstm, w_out, w_up, w_conv,
           b_conv, w_down):
    depth = w_in.shape[0]
    bp = x_prompt.shape[0]
    d_ff2 = w_up.shape[2]
    dh = HEAD_DIM_B
    c_zero = jnp.zeros((bp, N_HEADS_B, dh, dh), F32)
    n_zero = jnp.zeros((bp, N_HEADS_B, dh), F32)
    m_zero = jnp.zeros((bp, N_HEADS_B), F32)
    buf_zero = jnp.zeros((bp, CONV_W - 1, d_ff2), F32)
    xp, xs = x_prompt, x_sample
    ck_t = cache_k_att.swapaxes(-1, -2)
    cv_t = cache_v_att.swapaxes(-1, -2)
    outs_p, outs_s = [], []
    stacks = _stacked_weights(norm_g, w_in, b_i, b_f, rel_table, g_att, g_mlstm, w_out, w_up, w_conv, b_conv,
                              w_down)
    for l in range(depth):
        w = dict(stacks, layer=l)
        xp, *st_p = _layer(xp, None, None, c_zero, n_zero, m_zero, buf_zero, w)
        xs, *st_s = _layer(xs, ck_t, cv_t, state_mlstm_c[l], state_mlstm_n[l],
                           state_mlstm_m[l], cache_ffn_conv[l], w)
        outs_p.append(st_p)
        outs_s.append(st_s)
    stack = lambda outs, i: jnp.stack([o[i] for o in outs])
    return (xp, xs) + tuple(stack(outs_p, i) for i in range(6)) + tuple(stack(outs_s, i) for i in range(6))
```

```python
import functools
import math

import jax
import jax.numpy as jnp
from jax import lax
from jax.experimental import pallas as pl
from jax.experimental.pallas import tpu as pltpu

F32 = jnp.float32
BF16 = jnp.bfloat16

CHUNK = 64
LEFT_CHUNKS = 8
ATT_WINDOW = LEFT_CHUNKS * CHUNK
REL_CLIP = 256
N_HEADS_A = 8
HEAD_DIM_A = 64
WIDTH_A = N_HEADS_A * HEAD_DIM_A
N_HEADS_B = 4
HEAD_DIM_B = 128
WIDTH_B = N_HEADS_B * HEAD_DIM_B
CONV_W = 3
EPS = 1e-6
NEG = -1e30
LOG2E = math.log2(math.e)

LANES = 128
SUBLANES = 8
VMEM_LIMIT_BYTES = 56 * 1024 * 1024

INPROJ_ROWS = 1024
ATT_ROWS = ATT_WINDOW
ATT_GROUP = 4 * CHUNK
ATT_BAND = ATT_GROUP + ATT_WINDOW
BIAS_W = 1024
PAIR_W = 2 * HEAD_DIM_A
ATT_STEP_BATCHES = 2
MLSTM_BLOCK = 256
MLSTM_BATCHES = 2
MLSTM_EXT_PAD = 16
FFN_ROWS = 512
FFN_COLS = 256


def _params(n_axes):
    return pltpu.CompilerParams(dimension_semantics=("arbitrary",) * n_axes,
                                vmem_limit_bytes=VMEM_LIMIT_BYTES)


def _layer_spec(stacked, layer):
    rest = (0,) * (stacked.ndim - 1)
    return pl.BlockSpec((None,) + stacked.shape[1:], lambda *_: (layer,) + rest, pipeline_mode=pl.Buffered(1))


def _rms(x):
    return x * lax.rsqrt(jnp.mean(x * x, axis=-1, keepdims=True) + EPS)


def _dot(a, b):
    return jnp.dot(a, b, preferred_element_type=F32)


def _dot_nt(a, b):
    return lax.dot_general(a, b, (((1,), (1,)), ((), ())), preferred_element_type=F32)


def _inproj_body(x_ref, g_ref, wm_ref, wft_ref, gb_ref,
                 qa_ref, va_ref, kb_ref, ft_ref, gcb_ref, gcx_ref, *, seg):
    n_feat = ft_ref.shape[0]
    tm = x_ref.shape[0]
    hb = (_rms(x_ref[...]) * g_ref[0:1, :]).astype(BF16)
    zf = _dot_nt(wft_ref[...], hb)
    ft_ref[...] = zf[:n_feat].astype(BF16)
    pre = zf[n_feat:n_feat + 2 * N_HEADS_B] + zf[n_feat + 2 * N_HEADS_B:] + gb_ref[...]
    lf = jnp.minimum(pre, 0.0) - jnp.log(1.0 + jnp.exp(-jnp.abs(pre)))
    pos = lax.broadcasted_iota(jnp.int32, pre.shape, 1) & (seg - 1)
    k = 1
    while k < seg:
        lf = lf + jnp.where(pos >= k, pltpu.roll(lf, k, 1), 0.0)
        k *= 2
    row = lax.broadcasted_iota(jnp.int32, pre.shape, 0)
    gcb = jnp.where(row < N_HEADS_B, pre, lf)
    gcb_ref[...] = gcb
    c = gcb[:N_HEADS_B] - gcb[N_HEADS_B:]
    gcx_ref[...] = jnp.concatenate([c, jnp.zeros((LANES - N_HEADS_B, tm), F32)], axis=0).T
    for c_idx, (o_ref, scale) in enumerate(((qa_ref, HEAD_DIM_A ** -0.5 * LOG2E), (va_ref, None),
                                            (kb_ref, HEAD_DIM_B ** -0.5))):
        z = _dot(hb, wm_ref[:, c_idx * WIDTH_A:(c_idx + 1) * WIDTH_A])
        if scale is not None:
            z = z * scale
        o_ref[...] = z.astype(BF16)


def _inproj(x2d, layer, ng, wm, wft, gb, seg):
    t, d = x2d.shape
    tm = min(INPROJ_ROWS, t)
    assert t % tm == 0 and tm % seg == 0 and seg & (seg - 1) == 0
    n_feat = wft.shape[1] - 4 * N_HEADS_B
    row_spec = lambda w: pl.BlockSpec((tm, w), lambda i: (i, 0))
    col_spec = lambda r: pl.BlockSpec((r, tm), lambda i: (0, i))
    out_shape = [jax.ShapeDtypeStruct((t, WIDTH_A), BF16)] * 3 + [
        jax.ShapeDtypeStruct((n_feat, t), BF16), jax.ShapeDtypeStruct((2 * N_HEADS_B, t), F32),
        jax.ShapeDtypeStruct((t, LANES), F32)]
    out_specs = [row_spec(WIDTH_A)] * 3 + [col_spec(n_feat), col_spec(2 * N_HEADS_B), row_spec(LANES)]
    return pl.pallas_call(
        functools.partial(_inproj_body, seg=seg), grid=(t // tm,),
        in_specs=[row_spec(d)] + [_layer_spec(a, layer) for a in (ng, wm, wft, gb)],
        out_specs=out_specs, out_shape=out_shape, compiler_params=_params(1), name="inproj",
    )(x2d, ng, wm, wft, gb)


def _mlstm_body(*refs, n_valid, nbs):
    k_ref, gcx_ref = refs[:2]
    ft_refs = refs[2:2 + 3 * nbs]
    gcb_refs = refs[2 + 3 * nbs:2 + 4 * nbs]
    (g_ref, c0_ref, n0_ref, m0_ref, h_ref, c_out_ref, n_out_ref, m_out_ref, ct_sc, m_sc) = refs[2 + 4 * nbs:]
    j = pl.program_id(1)
    blk = k_ref.shape[1]
    dh = HEAD_DIM_B
    ext_rows = ct_sc.shape[2]

    @pl.when(j == 0)
    def _load_state():
        m_sc[...] = m0_ref[...]
        for bi in range(nbs):
            for hd in range(N_HEADS_B):
                ct_sc[bi, hd, :dh, :] = c0_ref[bi, hd].T
                ct_sc[bi, hd, dh:, :] = jnp.concatenate([n0_ref[bi, hd], jnp.zeros((ext_rows - dh - 1, dh), F32)], axis=0)

    row = lax.broadcasted_iota(jnp.int32, (blk, blk), 0)
    col = lax.broadcasted_iota(jnp.int32, (blk, blk), 1)
    valid = row <= col
    if n_valid < blk:
        valid = valid & (row < n_valid)
    lane1 = lax.broadcasted_iota(jnp.int32, (1, blk), 1)
    r_tail = lax.broadcasted_iota(jnp.int32, (ext_rows - dh, blk), 0)
    ones_tail = jnp.where(r_tail == 0, 1.0, 0.0).astype(BF16)
    last = n_valid - 1
    for bi in range(nbs):
        qt_ref, vt_ref, ot_ref = ft_refs[3 * bi:3 * bi + 3]
        for hd in range(N_HEADS_B):
            s_idx = bi * N_HEADS_B + hd
            sl = slice(hd * dh, (hd + 1) * dh)
            qt = qt_ref[sl, :]
            kk = k_ref[bi, :, sl]
            ig_r = gcb_refs[bi][hd:hd + 1, :]
            b_r = gcb_refs[bi][N_HEADS_B + hd:N_HEADS_B + hd + 1, :]
            m_prev = m_sc[s_idx:s_idx + 1, 0:1]
            dmat = jnp.where(valid, gcx_ref[bi, :, hd:hd + 1] + b_r, NEG)
            inter = b_r + m_prev
            m_t = jnp.maximum(inter, jnp.max(dmat, axis=0, keepdims=True))
            w_intra = jnp.exp(dmat - m_t)
            w_inter = jnp.exp(inter - m_t)
            ct_old = ct_sc[bi, hd]
            kc = _dot(jnp.concatenate([kk, ct_old.astype(BF16)], axis=0), qt)
            s_t = (kc[:blk] * w_intra).astype(BF16)
            vext = jnp.concatenate([vt_ref[sl, :], ones_tail], axis=0)
            ext = _dot(vext, s_t) + w_inter * kc[blk:]
            hh = ext[:dh] / jnp.maximum(jnp.abs(ext[dh:dh + 1]), jnp.exp(-m_t))
            hh = hh * lax.rsqrt(jnp.mean(hh * hh, axis=0, keepdims=True) + EPS)
            gate = 1.0 / (1.0 + jnp.exp(-ot_ref[sl, :].astype(F32)))
            h_ref[bi, :, sl] = ((hh * gate).T * g_ref[:, sl]).astype(BF16)
            m_new = m_t[:, last:last + 1]
            b_last = b_r[:, last:last + 1]
            w_s = jnp.exp(b_last - b_r + ig_r - m_new)
            if n_valid < blk:
                w_s = jnp.where(lane1 < n_valid, w_s, 0.0)
            decay = jnp.exp(b_last + m_prev - m_new)
            vtw = (vext.astype(F32) * w_s).astype(BF16)
            ct_sc[bi, hd] = decay * ct_old + _dot(vtw, kk)
            m_sc[s_idx:s_idx + 1, :] = jnp.broadcast_to(m_new, (1, m_sc.shape[1]))

    @pl.when(j == pl.num_programs(1) - 1)
    def _store_state():
        m_out_ref[...] = m_sc[...]
        for bi in range(nbs):
            for hd in range(N_HEADS_B):
                c_out_ref[bi, hd] = ct_sc[bi, hd, :dh, :].T
                n_out_ref[bi, hd] = ct_sc[bi, hd, dh:dh + 1, :]


def _mlstm(k, ft, gcb, gcx, layer, g, c0, n0, m0, blk, n_valid):
    nb, seq, _ = k.shape
    nblk = seq // blk
    nbs = math.gcd(nb, MLSTM_BATCHES)
    dh = HEAD_DIM_B
    tok = lambda w: pl.BlockSpec((nbs, blk, w), lambda bg, j: (bg, j, 0))
    col_blk = lambda k_: (lambda bg, j: (bg * nbs + k_) * nblk + j)
    ft_specs = [pl.BlockSpec((WIDTH_B, blk), lambda bg, j, r=r, f=col_blk(k_): (r, f(bg, j)))
                for k_ in range(nbs) for r in (1, 2, 3)]
    gcb_specs = [pl.BlockSpec((2 * N_HEADS_B, blk), lambda bg, j, f=col_blk(k_): (0, f(bg, j))) for k_ in range(nbs)]
    cspec = pl.BlockSpec((nbs, N_HEADS_B, dh, dh), lambda bg, j: (bg, 0, 0, 0))
    nspec = pl.BlockSpec((nbs, N_HEADS_B, 1, dh), lambda bg, j: (bg, 0, 0, 0))
    mspec = pl.BlockSpec((nbs * N_HEADS_B, LANES), lambda bg, j: (bg, 0))
    return pl.pallas_call(
        functools.partial(_mlstm_body, n_valid=n_valid, nbs=nbs), grid=(nb // nbs, nblk),
        in_specs=[tok(WIDTH_B), tok(LANES)] + ft_specs + gcb_specs + [_layer_spec(g, layer), cspec, nspec, mspec],
        out_specs=[tok(WIDTH_B), cspec, nspec, mspec],
        out_shape=[jax.ShapeDtypeStruct(k.shape, BF16), jax.ShapeDtypeStruct(c0.shape, F32),
                   jax.ShapeDtypeStruct(n0.shape, F32), jax.ShapeDtypeStruct(m0.shape, F32)],
        scratch_shapes=[pltpu.VMEM((nbs, N_HEADS_B, dh + (MLSTM_EXT_PAD if blk > LANES else dh), dh), F32),
                        pltpu.VMEM((nbs * N_HEADS_B, LANES), F32)],
        compiler_params=_params(2), name="mlstm",
    )(k, gcx, *([ft] * (3 * nbs)), *([gcb] * nbs), g, c0, n0, m0)


def _toeplitz_bias(gext_row, rows):
    gx = jnp.broadcast_to(gext_row, (rows, BIAS_W))
    return pltpu.roll(gx, 0, 1, stride=1, stride_axis=0)


def _attn_prompt_body(q_ref, kp_ref, kc_ref, vp_ref, vc_ref, gext_ref, gatt_ref, o_ref, bias_sc, o_sc):
    b = pl.program_id(0)
    i = pl.program_id(1)
    n_pairs = N_HEADS_A // 2

    @pl.when((b == 0) & (i == 0))
    def _init_bias():
        r = lax.broadcasted_iota(jnp.int32, (ATT_GROUP, BIAS_W), 0)
        c = lax.broadcasted_iota(jnp.int32, (ATT_GROUP, BIAS_W), 1)
        first = r - (r & (CHUNK - 1))
        ok = (c >= first) & (c < first + ATT_WINDOW + CHUNK)
        for hd in range(N_HEADS_A):
            t = _toeplitz_bias(gext_ref[hd:hd + 1, :], ATT_GROUP) * LOG2E
            bias_sc[hd // 2, (hd % 2) * ATT_GROUP:(hd % 2 + 1) * ATT_GROUP, :] = jnp.where(ok, t, NEG)[:, :ATT_BAND]

    lane = lax.broadcasted_iota(jnp.int32, (ATT_GROUP, PAIR_W), 1)
    low = lane < HEAD_DIM_A

    def step(has_prev):
        for g in range(ATT_ROWS // ATT_GROUP):
            r0 = g * ATT_GROUP
            n_prev = ATT_WINDOW - r0 if has_prev else 0
            n_cur = r0 + ATT_GROUP
            for p in range(n_pairs):
                ps = slice(p * PAIR_W, (p + 1) * PAIR_W)
                qp = q_ref[r0:r0 + ATT_GROUP, ps]
                zero = jnp.zeros_like(qp)
                qq = jnp.concatenate([jnp.where(low, qp, zero), jnp.where(low, zero, qp)], axis=0)
                s_cur = _dot(qq, kc_ref[ps, :n_cur])
                bias = bias_sc[p]
                if has_prev:
                    s = jnp.concatenate([_dot(qq, kp_ref[ps, r0:]), s_cur], axis=1) + bias
                else:
                    s = s_cur + bias[:, ATT_BAND - n_cur:]
                e = jnp.exp2(s - jnp.max(s, axis=-1, keepdims=True))
                l = jnp.sum(e, axis=-1, keepdims=True)
                eb = e.astype(BF16)
                o = _dot(eb[:, n_prev:], vc_ref[:n_cur, ps])
                if has_prev:
                    o = o + _dot(eb[:, :n_prev], vp_ref[r0:, ps])
                o = o / l
                o_sc[:, ps] = jnp.where(low, o[:ATT_GROUP], o[ATT_GROUP:])
            o_ref[r0:r0 + ATT_GROUP, :] = (_rms(o_sc[...]) * gatt_ref[...]).astype(BF16)

    @pl.when(i == 0)
    def _first_block():
        step(False)

    @pl.when(i > 0)
    def _later_block():
        step(True)


def _attn_prompt(qa, kt, va, layer, gext, gatt, nb):
    t = qa.shape[0]
    nblk = t // (nb * ATT_ROWS)
    cur = pl.BlockSpec((ATT_ROWS, WIDTH_A), lambda b, i: (b * nblk + i, 0))
    prev = pl.BlockSpec((ATT_ROWS, WIDTH_A), lambda b, i: (b * nblk + jnp.maximum(i - 1, 0), 0))
    cur_t = pl.BlockSpec((WIDTH_A, ATT_ROWS), lambda b, i: (0, b * nblk + i))
    prev_t = pl.BlockSpec((WIDTH_A, ATT_ROWS), lambda b, i: (0, b * nblk + jnp.maximum(i - 1, 0)))
    return pl.pallas_call(
        _attn_prompt_body, grid=(nb, nblk),
        in_specs=[cur, prev_t, cur_t, prev, cur, _layer_spec(gext, layer), _layer_spec(gatt, layer)],
        out_specs=cur, out_shape=jax.ShapeDtypeStruct((t, WIDTH_A), BF16),
        scratch_shapes=[pltpu.VMEM((N_HEADS_A // 2, 2 * ATT_GROUP, ATT_BAND), F32),
                        pltpu.VMEM((ATT_GROUP, WIDTH_A), F32)],
        compiler_params=_params(2), name="attn_prompt",
    )(qa, kt, kt, va, va, gext, gatt)


def _attn_step_body(q_ref, kn_ref, vn_ref, ck_ref, cv_ref, gext_ref, gatt_ref, o_ref, o_sc):
    nbs, _, s_new = kn_ref.shape
    l_cache = ck_ref.shape[3]
    for hd in range(N_HEADS_A):
        sl = slice(hd * HEAD_DIM_A, (hd + 1) * HEAD_DIM_A)
        bias = _toeplitz_bias(gext_ref[hd:hd + 1, :], s_new) * LOG2E
        for bi in range(nbs):
            rows = slice(bi * s_new, (bi + 1) * s_new)
            q = q_ref[rows, sl]
            s1 = _dot(q, ck_ref[bi, hd].astype(BF16)) + bias[:, :l_cache]
            s2 = _dot(q, kn_ref[bi, sl, :]) + bias[:, l_cache:l_cache + s_new]
            m = jnp.maximum(jnp.max(s1, axis=-1, keepdims=True), jnp.max(s2, axis=-1, keepdims=True))
            p1 = jnp.exp2(s1 - m)
            p2 = jnp.exp2(s2 - m)
            l = jnp.sum(p1, axis=-1, keepdims=True) + jnp.sum(p2, axis=-1, keepdims=True)
            o = _dot_nt(p1.astype(BF16), cv_ref[bi, hd].astype(BF16)) + _dot(p2.astype(BF16), vn_ref[rows, sl])
            o_sc[rows, sl] = o / l
    o_ref[...] = (_rms(o_sc[...]) * gatt_ref[...]).astype(BF16)


def _attn_step(qa, kt_new, va, ck_t, cv_t, layer, gext, gatt):
    _, nb, nh, dh, l_cache = ck_t.shape
    s_new = qa.shape[0] // nb
    nbs = math.gcd(nb, ATT_STEP_BATCHES)
    tok = pl.BlockSpec((nbs * s_new, WIDTH_A), lambda b: (b, 0))
    cache = pl.BlockSpec((None, nbs, nh, dh, l_cache), lambda b: (layer, b, 0, 0, 0))
    return pl.pallas_call(
        _attn_step_body, grid=(nb // nbs,),
        in_specs=[tok, pl.BlockSpec((nbs, WIDTH_A, s_new), lambda b: (b, 0, 0)), tok, cache, cache,
                  _layer_spec(gext, layer), _layer_spec(gatt, layer)],
        out_specs=tok, out_shape=jax.ShapeDtypeStruct(qa.shape, BF16),
        scratch_shapes=[pltpu.VMEM((nbs * s_new, WIDTH_A), F32)],
        compiler_params=_params(1), name="attn_step",
    )(qa, kt_new, va, ck_t, cv_t, gext, gatt)


GELU_C = math.sqrt(2.0 / math.pi)
GELU_A = 0.044715


def _gelu_gate(x, up):
    e = jnp.exp2(x * ((-2.0 * GELU_C * GELU_A * LOG2E) * (x * x) + (-2.0 * GELU_C * LOG2E)))
    return (x * up) / (1.0 + e)


def _shift_rows(u, prev_tile, k):
    tm, w = u.shape
    u3 = jnp.concatenate([prev_tile[None], u.reshape(tm // SUBLANES, SUBLANES, w)], axis=0)
    r = pltpu.roll(u3, k, 1)
    sub = lax.broadcasted_iota(jnp.int32, (1, SUBLANES, w), 1)
    return jnp.where(sub < k, r[:-1], r[1:]).reshape(tm, w)


def _ffn_body(x_ref, att_ref, ml_ref, ng_ref, wo_ref, wu_ref, wc_ref, bc_ref, wd_ref, init_ref,
              y_ref, tail_ref, carry_sc, a_sc, *, seq_rows):
    tm = x_ref.shape[0]
    d_ff = wd_ref.shape[0]
    multi_seq = seq_rows < tm

    if not multi_seq:
        @pl.when(pl.program_id(0) % (seq_rows // tm) == 0)
        def _seq_start():
            carry_sc[:SUBLANES - (CONV_W - 1), :] = jnp.zeros((SUBLANES - (CONV_W - 1), carry_sc.shape[1]), F32)
            carry_sc[SUBLANES - (CONV_W - 1):, :] = init_ref[...]

    mix = _dot(att_ref[...], wo_ref[:WIDTH_A, :]) + _dot(ml_ref[...], wo_ref[WIDTH_A:, :])
    x1 = x_ref[...] + _rms(mix) * ng_ref[1:2, :]
    h2 = (_rms(x1) * ng_ref[2:3, :]).astype(BF16)
    if multi_seq:
        pos = lax.broadcasted_iota(jnp.int32, (tm, FFN_COLS), 0) & (seq_rows - 1)

    def conv(cols):
        u = _dot(h2, wu_ref[:, cols])
        if multi_seq:
            fix = init_ref[:, cols]
            um2 = jnp.where(pos < 2, fix, pltpu.roll(u, 2, 0))
            um1 = jnp.where(pos < 1, pltpu.roll(fix, tm - 1, 0), pltpu.roll(u, 1, 0))
            tail_ref[:, cols] = u
        else:
            prev = carry_sc[:, cols]
            um2 = _shift_rows(u, prev, 2)
            um1 = _shift_rows(u, prev, 1)
            carry_sc[:, cols] = u[tm - SUBLANES:, :]
            tail_ref[:, cols] = u[tm - (CONV_W - 1):, :]
        return bc_ref[:, cols] + um2 * wc_ref[0:1, cols] + um1 * wc_ref[1:2, cols] + u * wc_ref[2:3, cols]

    for c in range(d_ff // FFN_COLS):
        gate = conv(slice(c * FFN_COLS, (c + 1) * FFN_COLS))
        up = conv(slice(d_ff + c * FFN_COLS, d_ff + (c + 1) * FFN_COLS))
        a_sc[:, c * FFN_COLS:(c + 1) * FFN_COLS] = _gelu_gate(gate, up).astype(BF16)
    f = _dot(a_sc[...], wd_ref[...])
    y_ref[...] = x1 + _rms(f) * ng_ref[3:4, :]


def _ffn(x2d, att, ml, layer, ng, wo, wu, wc, bc, wd, init, seq_rows):
    t, d = x2d.shape
    d_ff = wd.shape[1]
    tm = min(FFN_ROWS, t)
    assert t % tm == 0 and d_ff % FFN_COLS == 0
    multi_seq = seq_rows < tm
    if multi_seq:
        assert tm % seq_rows == 0 and t == tm and seq_rows & (seq_rows - 1) == 0
        init_spec = pl.BlockSpec(init.shape, lambda i: (0, 0))
        tail_spec = pl.BlockSpec((tm, 2 * d_ff), lambda i: (0, 0))
        tail_shape = jax.ShapeDtypeStruct((tm, 2 * d_ff), F32)
    else:
        assert seq_rows % tm == 0
        per_seq = seq_rows // tm
        init_spec = pl.BlockSpec((None, CONV_W - 1, 2 * d_ff), lambda i: (i // per_seq, 0, 0))
        tail_spec = init_spec
        tail_shape = jax.ShapeDtypeStruct(init.shape, F32)
    row_spec = lambda w: pl.BlockSpec((tm, w), lambda i: (i, 0))
    return pl.pallas_call(
        functools.partial(_ffn_body, seq_rows=seq_rows), grid=(t // tm,),
        in_specs=([row_spec(d), row_spec(WIDTH_A), row_spec(WIDTH_B)]
                  + [_layer_spec(a, layer) for a in (ng, wo, wu, wc, bc, wd)] + [init_spec]),
        out_specs=[row_spec(d), tail_spec],
        out_shape=[jax.ShapeDtypeStruct((t, d), F32), tail_shape],
        scratch_shapes=[pltpu.VMEM((SUBLANES, 2 * d_ff), F32), pltpu.VMEM((tm, d_ff), BF16)],
        compiler_params=_params(1), name="ffn",
    )(x2d, att, ml, ng, wo, wu, wc, bc, wd, init)


def _bias_rows(rel_table_l):
    n_far = ATT_WINDOW - REL_CLIP + 1
    far = jnp.broadcast_to(rel_table_l[:, 2 * REL_CLIP:], (N_HEADS_A, n_far))
    lo = ATT_WINDOW + REL_CLIP - (ATT_BAND - 1)
    near = rel_table_l[:, lo:2 * REL_CLIP][:, ::-1]
    wrap = jnp.broadcast_to(rel_table_l[:, 2 * REL_CLIP:], (N_HEADS_A, BIAS_W - ATT_BAND))
    return jnp.concatenate([far, near, wrap], axis=1)


def _pad_seq(a, seq_pad):
    return a if a.shape[1] == seq_pad else jnp.pad(a, ((0, 0), (0, seq_pad - a.shape[1]), (0, 0)))


def _pad_cols(a, nb, seq, seq_pad):
    if seq == seq_pad:
        return a
    return jnp.pad(a.reshape(-1, nb, seq), ((0, 0), (0, 0), (0, seq_pad - seq))).reshape(a.shape[0], -1)


def _layer(x, k_cache, v_cache, c0, n0, m0, conv_buf, w):
    nb, seq, d = x.shape
    x2d = x.reshape(nb * seq, d)
    blk = MLSTM_BLOCK if seq % MLSTM_BLOCK == 0 else LANES
    seg = min(blk, seq)
    l = w["layer"]
    qa, va, kb, ft, gcb, gcx = _inproj(x2d, l, w["norm_g"], w["w_main"], w["w_ft"], w["gate_bias"], seg)

    keep = min(ATT_WINDOW, seq) if k_cache is None else seq
    k_tail = jnp.stack([ft[:WIDTH_A, (b + 1) * seq - keep:(b + 1) * seq] for b in range(nb)])
    if k_cache is None:
        att = _attn_prompt(qa, ft, va, l, w["gext"], w["g_att"], nb)
    else:
        att = _attn_step(qa, k_tail, va, k_cache, v_cache, l, w["gext"], w["g_att"])
    new_k = k_tail.reshape(nb, N_HEADS_A, HEAD_DIM_A, keep).astype(F32).swapaxes(-1, -2)
    new_v = va.reshape(nb, seq, WIDTH_A)[:, seq - keep:].reshape(nb, keep, N_HEADS_A, HEAD_DIM_A)
    new_v = new_v.transpose(0, 2, 1, 3).astype(F32)

    seq_pad = -(-seq // blk) * blk
    m0b = jnp.broadcast_to(m0.reshape(-1, 1), (nb * N_HEADS_B, LANES))
    r3 = lambda a: _pad_seq(a.reshape(nb, seq, -1), seq_pad)
    hb, c1, n1, m_b = _mlstm(r3(kb), _pad_cols(ft, nb, seq, seq_pad), _pad_cols(gcb, nb, seq, seq_pad), r3(gcx),
                             l, w["g_mlstm"], c0, n0[:, :, None, :], m0b, blk, seq - (seq_pad - blk))
    hb = hb[:, :seq].reshape(nb * seq, WIDTH_B)
    n1, m1 = n1[:, :, 0, :], m_b[:, 0].reshape(nb, N_HEADS_B)

    tm = min(FFN_ROWS, nb * seq)
    if seq < tm:
        init = jnp.pad(conv_buf, ((0, 0), (0, seq - (CONV_W - 1)), (0, 0))).reshape(nb * seq, -1)
        y, u = _ffn(x2d, att, hb, l, w["norm_g"], w["w_out"], w["w_up"], w["w_conv"], w["b_conv"],
                    w["w_down"], init, seq)
        new_buf = u.reshape(nb, seq, -1)[:, seq - (CONV_W - 1):]
    else:
        y, new_buf = _ffn(x2d, att, hb, l, w["norm_g"], w["w_out"], w["w_up"], w["w_conv"], w["b_conv"],
                          w["w_down"], conv_buf, seq)
    return y.reshape(nb, seq, d), new_k, new_v, c1, n1, m1, new_buf


def _stacked_weights(norm_g, w_in, b_i, b_f, rel_table, g_att, g_mlstm, w_out, w_up, w_conv, b_conv, w_down):
    wa, wb = WIDTH_A, WIDTH_B
    assert wa == wb
    col = lambda k: slice(k * wa, (k + 1) * wa)
    wi = w_in.astype(BF16)
    w_main = jnp.concatenate([wi[..., col(k)] for k in (0, 2, 4)], axis=-1)
    wg = w_in[..., 7 * wa:]
    wg_hi = wg.astype(BF16)
    wg_lo = (wg - wg_hi.astype(F32)).astype(BF16)
    w_ft = jnp.concatenate([wi[..., col(k)] for k in (1, 3, 5, 6)] + [wg_hi, wg_lo], axis=-1).swapaxes(1, 2)
    return dict(
        norm_g=norm_g, w_main=w_main, w_ft=w_ft, gate_bias=jnp.concatenate([b_i, b_f], axis=-1)[..., None],
        gext=jax.vmap(_bias_rows)(rel_table), g_att=g_att[:, None, :], g_mlstm=g_mlstm[:, None, :],
        w_out=w_out.astype(BF16), w_up=w_up.astype(BF16), w_conv=w_conv, b_conv=b_conv[:, None, :],
        w_down=w_down.astype(BF16))


def kernel(x_prompt, x_sample, cache_k_att, cache_v_att, state_mlstm_c, state_mlstm_n, state_mlstm_m,
           cache_ffn_conv, norm_g, w_in, b_i, b_f, rel_table, g_att, g_mlstm, w_out, w_up, w_conv,
           b_conv, w_down):
    depth = w_in.shape[0]
    bp = x_prompt.shape[0]
    d_ff2 = w_up.shape[2]
    dh = HEAD_DIM_B
    c_zero = jnp.zeros((bp, N_HEADS_B, dh, dh), F32)
    n_zero = jnp.zeros((bp, N_HEADS_B, dh), F32)
    m_zero = jnp.zeros((bp, N_HEADS_B), F32)
    buf_zero = jnp.zeros((bp, CONV_W - 1, d_ff2), F32)
    xp, xs = x_prompt, x_sample
    ck_t = cache_k_att.swapaxes(-1, -2)
    cv_t = cache_v_att.swapaxes(-1, -2)
    outs_p, outs_s = [], []
    stacks = _stacked_weights(norm_g, w_in, b_i, b_f, rel_table, g_att, g_mlstm, w_out, w_up, w_conv, b_conv,
                              w_down)
    for l in range(depth):
        w = dict(stacks, layer=l)
        xp, *st_p = _layer(xp, None, None, c_zero, n_zero, m_zero, buf_zero, w)
        xs, *st_s = _layer(xs, ck_t, cv_t, state_mlstm_c[l], state_mlstm_n[l],
                           state_mlstm_m[l], cache_ffn_conv[l], w)
        outs_p.append(st_p)
        outs_s.append(st_s)
    stack = lambda outs, i: jnp.stack([o[i] for o in outs])
    return (xp, xs) + tuple(stack(outs_p, i) for i in range(6)) + tuple(stack(outs_s, i) for i in range(6))
```

```python
import functools
import math

import jax
import jax.numpy as jnp
from jax import lax
from jax.experimental import pallas as pl
from jax.experimental.pallas import tpu as pltpu

F32 = jnp.float32
BF16 = jnp.bfloat16

CHUNK = 64
LEFT_CHUNKS = 8
ATT_WINDOW = LEFT_CHUNKS * CHUNK
REL_CLIP = 256
N_HEADS_A = 8
HEAD_DIM_A = 64
WIDTH_A = N_HEADS_A * HEAD_DIM_A
N_HEADS_B = 4
HEAD_DIM_B = 128
WIDTH_B = N_HEADS_B * HEAD_DIM_B
CONV_W = 3
EPS = 1e-6
NEG = -1e30
LOG2E = math.log2(math.e)

LANES = 128
SUBLANES = 8
VMEM_LIMIT_BYTES = 56 * 1024 * 1024

INPROJ_ROWS = 1024
ATT_ROWS = ATT_WINDOW
ATT_GROUP = 4 * CHUNK
ATT_BAND = ATT_GROUP + ATT_WINDOW
BIAS_W = 1024
PAIR_W = 2 * HEAD_DIM_A
ATT_STEP_BATCHES = 2
MLSTM_BLOCK = 256
MLSTM_BATCHES = 2
MLSTM_EXT_PAD = 16
FFN_ROWS = 512
FFN_COLS = 256


def _params(n_axes):
    return pltpu.CompilerParams(dimension_semantics=("arbitrary",) * n_axes,
                                vmem_limit_bytes=VMEM_LIMIT_BYTES)


def _layer_spec(stacked, layer):
    rest = (0,) * (stacked.ndim - 1)
    return pl.BlockSpec((None,) + stacked.shape[1:], lambda *_: (layer,) + rest, pipeline_mode=pl.Buffered(1))


def _rms(x):
    return x * lax.rsqrt(jnp.mean(x * x, axis=-1, keepdims=True) + EPS)


def _dot(a, b):
    return jnp.dot(a, b, preferred_element_type=F32)


def _dot_nt(a, b):
    return lax.dot_general(a, b, (((1,), (1,)), ((), ())), preferred_element_type=F32)


def _inproj_body(x_ref, g_ref, wm_ref, wft_ref, gb_ref,
                 qa_ref, va_ref, kb_ref, ft_ref, gcb_ref, gcx_ref, *, seg):
    n_feat = ft_ref.shape[0]
    tm = x_ref.shape[0]
    hb = (_rms(x_ref[...]) * g_ref[0:1, :]).astype(BF16)
    zf = _dot_nt(wft_ref[...], hb)
    ft_ref[...] = zf[:n_feat].astype(BF16)
    pre = zf[n_feat:n_feat + 2 * N_HEADS_B] + zf[n_feat + 2 * N_HEADS_B:] + gb_ref[...]
    lf = jnp.minimum(pre, 0.0) - jnp.log(1.0 + jnp.exp(-jnp.abs(pre)))
    pos = lax.broadcasted_iota(jnp.int32, pre.shape, 1) & (seg - 1)
    k = 1
    while k < seg:
        lf = lf + jnp.where(pos >= k, pltpu.roll(lf, k, 1), 0.0)
        k *= 2
    row = lax.broadcasted_iota(jnp.int32, pre.shape, 0)
    gcb = jnp.where(row < N_HEADS_B, pre, lf)
    gcb_ref[...] = gcb
    c = gcb[:N_HEADS_B] - gcb[N_HEADS_B:]
    gcx_ref[...] = jnp.concatenate([c, jnp.zeros((LANES - N_HEADS_B, tm), F32)], axis=0).T
    for c_idx, (o_ref, scale) in enumerate(((qa_ref, HEAD_DIM_A ** -0.5 * LOG2E), (va_ref, None),
                                            (kb_ref, HEAD_DIM_B ** -0.5))):
        z = _dot(hb, wm_ref[:, c_idx * WIDTH_A:(c_idx + 1) * WIDTH_A])
        if scale is not None:
            z = z * scale
        o_ref[...] = z.astype(BF16)


def _inproj(x2d, layer, ng, wm, wft, gb, seg):
    t, d = x2d.shape
    tm = min(INPROJ_ROWS, t)
    assert t % tm == 0 and tm % seg == 0 and seg & (seg - 1) == 0
    n_feat = wft.shape[1] - 4 * N_HEADS_B
    row_spec = lambda w: pl.BlockSpec((tm, w), lambda i: (i, 0))
    col_spec = lambda r: pl.BlockSpec((r, tm), lambda i: (0, i))
    out_shape = [jax.ShapeDtypeStruct((t, WIDTH_A), BF16)] * 3 + [
        jax.ShapeDtypeStruct((n_feat, t), BF16), jax.ShapeDtypeStruct((2 * N_HEADS_B, t), F32),
        jax.ShapeDtypeStruct((t, LANES), F32)]
    out_specs = [row_spec(WIDTH_A)] * 3 + [col_spec(n_feat), col_spec(2 * N_HEADS_B), row_spec(LANES)]
    return pl.pallas_call(
        functools.partial(_inproj_body, seg=seg), grid=(t // tm,),
        in_specs=[row_spec(d)] + [_layer_spec(a, layer) for a in (ng, wm, wft, gb)],
        out_specs=out_specs, out_shape=out_shape, compiler_params=_params(1), name="inproj",
    )(x2d, ng, wm, wft, gb)


def _mlstm_body(*refs, n_valid, nbs):
    k_ref, gcx_ref = refs[:2]
    ft_refs = refs[2:2 + 3 * nbs]
    gcb_refs = refs[2 + 3 * nbs:2 + 4 * nbs]
    (g_ref, c0_ref, n0_ref, m0_ref, h_ref, c_out_ref, n_out_ref, m_out_ref, ct_sc, m_sc) = refs[2 + 4 * nbs:]
    j = pl.program_id(1)
    blk = k_ref.shape[1]
    dh = HEAD_DIM_B
    ext_rows = ct_sc.shape[2]

    @pl.when(j == 0)
    def _load_state():
        m_sc[...] = m0_ref[...]
        for bi in range(nbs):
            for hd in range(N_HEADS_B):
                ct_sc[bi, hd, :dh, :] = c0_ref[bi, hd].T
                ct_sc[bi, hd, dh:, :] = jnp.concatenate([n0_ref[bi, hd], jnp.zeros((ext_rows - dh - 1, dh), F32)], axis=0)

    row = lax.broadcasted_iota(jnp.int32, (blk, blk), 0)
    col = lax.broadcasted_iota(jnp.int32, (blk, blk), 1)
    valid = row <= col
    if n_valid < blk:
        valid = valid & (row < n_valid)
    lane1 = lax.broadcasted_iota(jnp.int32, (1, blk), 1)
    r_tail = lax.broadcasted_iota(jnp.int32, (ext_rows - dh, blk), 0)
    ones_tail = jnp.where(r_tail == 0, 1.0, 0.0).astype(BF16)
    last = n_valid - 1
    def weights(bi, hd):
        s_idx = bi * N_HEADS_B + hd
        ig_r = gcb_refs[bi][hd:hd + 1, :]
        b_r = gcb_refs[bi][N_HEADS_B + hd:N_HEADS_B + hd + 1, :]
        m_prev = m_sc[s_idx:s_idx + 1, 0:1]
        dmat = jnp.where(valid, gcx_ref[bi, :, hd:hd + 1] + b_r, NEG)
        inter = b_r + m_prev
        m_t = jnp.maximum(inter, jnp.max(dmat, axis=0, keepdims=True))
        w_intra = jnp.exp(dmat - m_t)
        w_inter = jnp.exp(inter - m_t)
        m_new = m_t[:, last:last + 1]
        b_last = b_r[:, last:last + 1]
        w_s = jnp.exp(b_last - b_r + ig_r - m_new)
        if n_valid < blk:
            w_s = jnp.where(lane1 < n_valid, w_s, 0.0)
        decay = jnp.exp(b_last + m_prev - m_new)
        m_sc[s_idx:s_idx + 1, :] = jnp.broadcast_to(m_new, (1, m_sc.shape[1]))
        return m_t, w_intra, w_inter, w_s, decay

    def mix(bi, hd, m_t, w_intra, w_inter, w_s, decay):
        qt_ref, vt_ref, ot_ref = ft_refs[3 * bi:3 * bi + 3]
        sl = slice(hd * dh, (hd + 1) * dh)
        qt = qt_ref[sl, :]
        kk = k_ref[bi, :, sl]
        ct_old = ct_sc[bi, hd]
        kc = _dot(jnp.concatenate([kk, ct_old.astype(BF16)], axis=0), qt)
        s_t = (kc[:blk] * w_intra).astype(BF16)
        vext = jnp.concatenate([vt_ref[sl, :], ones_tail], axis=0)
        ext = _dot(vext, s_t) + w_inter * kc[blk:]
        hh = ext[:dh] / jnp.maximum(jnp.abs(ext[dh:dh + 1]), jnp.exp(-m_t))
        hh = hh * lax.rsqrt(jnp.mean(hh * hh, axis=0, keepdims=True) + EPS)
        gate = 1.0 / (1.0 + jnp.exp(-ot_ref[sl, :].astype(F32)))
        h_ref[bi, :, sl] = ((hh * gate).T * g_ref[:, sl]).astype(BF16)
        vtw = (vext.astype(F32) * w_s).astype(BF16)
        ct_sc[bi, hd] = decay * ct_old + _dot(vtw, kk)

    items = [(bi, hd) for bi in range(nbs) for hd in range(N_HEADS_B)]
    w_next = weights(*items[0])
    for k, item in enumerate(items):
        w_now = w_next
        if k + 1 < len(items):
            w_next = weights(*items[k + 1])
        mix(*item, *w_now)

    @pl.when(j == pl.num_programs(1) - 1)
    def _store_state():
        m_out_ref[...] = m_sc[...]
        for bi in range(nbs):
            for hd in range(N_HEADS_B):
                c_out_ref[bi, hd] = ct_sc[bi, hd, :dh, :].T
                n_out_ref[bi, hd] = ct_sc[bi, hd, dh:dh + 1, :]


def _mlstm(k, ft, gcb, gcx, layer, g, c0, n0, m0, blk, n_valid):
    nb, seq, _ = k.shape
    nblk = seq // blk
    nbs = math.gcd(nb, MLSTM_BATCHES)
    dh = HEAD_DIM_B
    tok = lambda w: pl.BlockSpec((nbs, blk, w), lambda bg, j: (bg, j, 0))
    col_blk = lambda k_: (lambda bg, j: (bg * nbs + k_) * nblk + j)
    ft_specs = [pl.BlockSpec((WIDTH_B, blk), lambda bg, j, r=r, f=col_blk(k_): (r, f(bg, j)))
                for k_ in range(nbs) for r in (1, 2, 3)]
    gcb_specs = [pl.BlockSpec((2 * N_HEADS_B, blk), lambda bg, j, f=col_blk(k_): (0, f(bg, j))) for k_ in range(nbs)]
    cspec = pl.BlockSpec((nbs, N_HEADS_B, dh, dh), lambda bg, j: (bg, 0, 0, 0))
    nspec = pl.BlockSpec((nbs, N_HEADS_B, 1, dh), lambda bg, j: (bg, 0, 0, 0))
    mspec = pl.BlockSpec((nbs * N_HEADS_B, LANES), lambda bg, j: (bg, 0))
    return pl.pallas_call(
        functools.partial(_mlstm_body, n_valid=n_valid, nbs=nbs), grid=(nb // nbs, nblk),
        in_specs=[tok(WIDTH_B), tok(LANES)] + ft_specs + gcb_specs + [_layer_spec(g, layer), cspec, nspec, mspec],
        out_specs=[tok(WIDTH_B), cspec, nspec, mspec],
        out_shape=[jax.ShapeDtypeStruct(k.shape, BF16), jax.ShapeDtypeStruct(c0.shape, F32),
                   jax.ShapeDtypeStruct(n0.shape, F32), jax.ShapeDtypeStruct(m0.shape, F32)],
        scratch_shapes=[pltpu.VMEM((nbs, N_HEADS_B, dh + (MLSTM_EXT_PAD if blk > LANES else dh), dh), F32),
                        pltpu.VMEM((nbs * N_HEADS_B, LANES), F32)],
        compiler_params=_params(2), name="mlstm",
    )(k, gcx, *([ft] * (3 * nbs)), *([gcb] * nbs), g, c0, n0, m0)


def _toeplitz_bias(gext_row, rows):
    gx = jnp.broadcast_to(gext_row, (rows, BIAS_W))
    return pltpu.roll(gx, 0, 1, stride=1, stride_axis=0)


def _attn_prompt_body(q_ref, kp_ref, kc_ref, vp_ref, vc_ref, gext_ref, gatt_ref, o_ref, bias_sc, o_sc):
    b = pl.program_id(0)
    i = pl.program_id(1)
    n_pairs = N_HEADS_A // 2

    @pl.when((b == 0) & (i == 0))
    def _init_bias():
        r = lax.broadcasted_iota(jnp.int32, (ATT_GROUP, BIAS_W), 0)
        c = lax.broadcasted_iota(jnp.int32, (ATT_GROUP, BIAS_W), 1)
        first = r - (r & (CHUNK - 1))
        ok = (c >= first) & (c < first + ATT_WINDOW + CHUNK)
        for hd in range(N_HEADS_A):
            t = _toeplitz_bias(gext_ref[hd:hd + 1, :], ATT_GROUP) * LOG2E
            bias_sc[hd // 2, (hd % 2) * ATT_GROUP:(hd % 2 + 1) * ATT_GROUP, :] = jnp.where(ok, t, NEG)[:, :ATT_BAND]

    lane = lax.broadcasted_iota(jnp.int32, (ATT_GROUP, PAIR_W), 1)
    low = lane < HEAD_DIM_A

    def step(has_prev):
        def logits(g, p):
            r0 = g * ATT_GROUP
            n_cur = r0 + ATT_GROUP
            ps = slice(p * PAIR_W, (p + 1) * PAIR_W)
            qp = q_ref[r0:r0 + ATT_GROUP, ps]
            zero = jnp.zeros_like(qp)
            qq = jnp.concatenate([jnp.where(low, qp, zero), jnp.where(low, zero, qp)], axis=0)
            s_cur = _dot(qq, kc_ref[ps, :n_cur])
            if has_prev:
                return jnp.concatenate([_dot(qq, kp_ref[ps, r0:]), s_cur], axis=1) + bias_sc[p]
            return s_cur + bias_sc[p, :, ATT_BAND - n_cur:]

        def softmax(s):
            e = jnp.exp2(s - jnp.max(s, axis=-1, keepdims=True))
            return e.astype(BF16), jnp.sum(e, axis=-1, keepdims=True)

        def values(g, p, eb, l):
            r0 = g * ATT_GROUP
            n_prev = ATT_WINDOW - r0 if has_prev else 0
            n_cur = r0 + ATT_GROUP
            ps = slice(p * PAIR_W, (p + 1) * PAIR_W)
            o = _dot(eb[:, n_prev:], vc_ref[:n_cur, ps])
            if has_prev:
                o = o + _dot(eb[:, :n_prev], vp_ref[r0:, ps])
            o = o / l
            o_sc[g, :, ps] = jnp.where(low, o[:ATT_GROUP], o[ATT_GROUP:])
            if p == n_pairs - 1:
                o_ref[r0:r0 + ATT_GROUP, :] = (_rms(o_sc[g]) * gatt_ref[...]).astype(BF16)

        items = [(g, p) for g in range(ATT_ROWS // ATT_GROUP) for p in range(n_pairs)]
        s_next = logits(*items[0])
        for k, item in enumerate(items):
            s_now = s_next
            if k + 1 < len(items):
                s_next = logits(*items[k + 1])
            values(*item, *softmax(s_now))

    @pl.when(i == 0)
    def _first_block():
        step(False)

    @pl.when(i > 0)
    def _later_block():
        step(True)


def _attn_prompt(qa, kt, va, layer, gext, gatt, nb):
    t = qa.shape[0]
    nblk = t // (nb * ATT_ROWS)
    cur = pl.BlockSpec((ATT_ROWS, WIDTH_A), lambda b, i: (b * nblk + i, 0))
    prev = pl.BlockSpec((ATT_ROWS, WIDTH_A), lambda b, i: (b * nblk + jnp.maximum(i - 1, 0), 0))
    cur_t = pl.BlockSpec((WIDTH_A, ATT_ROWS), lambda b, i: (0, b * nblk + i))
    prev_t = pl.BlockSpec((WIDTH_A, ATT_ROWS), lambda b, i: (0, b * nblk + jnp.maximum(i - 1, 0)))
    return pl.pallas_call(
        _attn_prompt_body, grid=(nb, nblk),
        in_specs=[cur, prev_t, cur_t, prev, cur, _layer_spec(gext, layer), _layer_spec(gatt, layer)],
        out_specs=cur, out_shape=jax.ShapeDtypeStruct((t, WIDTH_A), BF16),
        scratch_shapes=[pltpu.VMEM((N_HEADS_A // 2, 2 * ATT_GROUP, ATT_BAND), F32),
                        pltpu.VMEM((ATT_ROWS // ATT_GROUP, ATT_GROUP, WIDTH_A), F32)],
        compiler_params=_params(2), name="attn_prompt",
    )(qa, kt, kt, va, va, gext, gatt)


def _attn_step_body(q_ref, kn_ref, vn_ref, ck_ref, cv_ref, gext_ref, gatt_ref, o_ref, o_sc):
    nbs, _, s_new = kn_ref.shape
    l_cache = ck_ref.shape[3]
    for hd in range(N_HEADS_A):
        sl = slice(hd * HEAD_DIM_A, (hd + 1) * HEAD_DIM_A)
        bias = _toeplitz_bias(gext_ref[hd:hd + 1, :], s_new) * LOG2E
        for bi in range(nbs):
            rows = slice(bi * s_new, (bi + 1) * s_new)
            q = q_ref[rows, sl]
            s1 = _dot(q, ck_ref[bi, hd].astype(BF16)) + bias[:, :l_cache]
            s2 = _dot(q, kn_ref[bi, sl, :]) + bias[:, l_cache:l_cache + s_new]
            m = jnp.maximum(jnp.max(s1, axis=-1, keepdims=True), jnp.max(s2, axis=-1, keepdims=True))
            p1 = jnp.exp2(s1 - m)
            p2 = jnp.exp2(s2 - m)
            l = jnp.sum(p1, axis=-1, keepdims=True) + jnp.sum(p2, axis=-1, keepdims=True)
            o = _dot_nt(p1.astype(BF16), cv_ref[bi, hd].astype(BF16)) + _dot(p2.astype(BF16), vn_ref[rows, sl])
            o_sc[rows, sl] = o / l
    o_ref[...] = (_rms(o_sc[...]) * gatt_ref[...]).astype(BF16)


def _attn_step(qa, kt_new, va, ck_t, cv_t, layer, gext, gatt):
    _, nb, nh, dh, l_cache = ck_t.shape
    s_new = qa.shape[0] // nb
    nbs = math.gcd(nb, ATT_STEP_BATCHES)
    tok = pl.BlockSpec((nbs * s_new, WIDTH_A), lambda b: (b, 0))
    cache = pl.BlockSpec((None, nbs, nh, dh, l_cache), lambda b: (layer, b, 0, 0, 0))
    return pl.pallas_call(
        _attn_step_body, grid=(nb // nbs,),
        in_specs=[tok, pl.BlockSpec((nbs, WIDTH_A, s_new), lambda b: (b, 0, 0)), tok, cache, cache,
                  _layer_spec(gext, layer), _layer_spec(gatt, layer)],
        out_specs=tok, out_shape=jax.ShapeDtypeStruct(qa.shape, BF16),
        scratch_shapes=[pltpu.VMEM((nbs * s_new, WIDTH_A), F32)],
        compiler_params=_params(1), name="attn_step",
    )(qa, kt_new, va, ck_t, cv_t, gext, gatt)


GELU_C = math.sqrt(2.0 / math.pi)
GELU_A = 0.044715


def _gelu_gate(x, up):
    e = jnp.exp2(x * ((-2.0 * GELU_C * GELU_A * LOG2E) * (x * x) + (-2.0 * GELU_C * LOG2E)))
    return (x * up) / (1.0 + e)


def _shift_rows(u, prev_tile, k):
    tm, w = u.shape
    u3 = jnp.concatenate([prev_tile[None], u.reshape(tm // SUBLANES, SUBLANES, w)], axis=0)
    r = pltpu.roll(u3, k, 1)
    sub = lax.broadcasted_iota(jnp.int32, (1, SUBLANES, w), 1)
    return jnp.where(sub < k, r[:-1], r[1:]).reshape(tm, w)


def _ffn_body(x_ref, att_ref, ml_ref, ng_ref, wo_ref, wu_ref, wc_ref, bc_ref, wd_ref, init_ref,
              y_ref, tail_ref, carry_sc, a_sc, *, seq_rows):
    tm = x_ref.shape[0]
    d_ff = wd_ref.shape[0]
    multi_seq = seq_rows < tm

    if not multi_seq:
        @pl.when(pl.program_id(0) % (seq_rows // tm) == 0)
        def _seq_start():
            carry_sc[:SUBLANES - (CONV_W - 1), :] = jnp.zeros((SUBLANES - (CONV_W - 1), carry_sc.shape[1]), F32)
            carry_sc[SUBLANES - (CONV_W - 1):, :] = init_ref[...]

    mix = _dot(att_ref[...], wo_ref[:WIDTH_A, :]) + _dot(ml_ref[...], wo_ref[WIDTH_A:, :])
    x1 = x_ref[...] + _rms(mix) * ng_ref[1:2, :]
    h2 = (_rms(x1) * ng_ref[2:3, :]).astype(BF16)
    if multi_seq:
        pos = lax.broadcasted_iota(jnp.int32, (tm, FFN_COLS), 0) & (seq_rows - 1)

    def conv(cols):
        u = _dot(h2, wu_ref[:, cols])
        if multi_seq:
            fix = init_ref[:, cols]
            um2 = jnp.where(pos < 2, fix, pltpu.roll(u, 2, 0))
            um1 = jnp.where(pos < 1, pltpu.roll(fix, tm - 1, 0), pltpu.roll(u, 1, 0))
            tail_ref[:, cols] = u
        else:
            prev = carry_sc[:, cols]
            um2 = _shift_rows(u, prev, 2)
            um1 = _shift_rows(u, prev, 1)
            carry_sc[:, cols] = u[tm - SUBLANES:, :]
            tail_ref[:, cols] = u[tm - (CONV_W - 1):, :]
        return bc_ref[:, cols] + um2 * wc_ref[0:1, cols] + um1 * wc_ref[1:2, cols] + u * wc_ref[2:3, cols]

    for c in range(d_ff // FFN_COLS):
        gate = conv(slice(c * FFN_COLS, (c + 1) * FFN_COLS))
        up = conv(slice(d_ff + c * FFN_COLS, d_ff + (c + 1) * FFN_COLS))
        a_sc[:, c * FFN_COLS:(c + 1) * FFN_COLS] = _gelu_gate(gate, up).astype(BF16)
    f = _dot(a_sc[...], wd_ref[...])
    y_ref[...] = x1 + _rms(f) * ng_ref[3:4, :]


def _ffn(x2d, att, ml, layer, ng, wo, wu, wc, bc, wd, init, seq_rows):
    t, d = x2d.shape
    d_ff = wd.shape[1]
    tm = min(FFN_ROWS, t)
    assert t % tm == 0 and d_ff % FFN_COLS == 0
    multi_seq = seq_rows < tm
    if multi_seq:
        assert tm % seq_rows == 0 and t == tm and seq_rows & (seq_rows - 1) == 0
        init_spec = pl.BlockSpec(init.shape, lambda i: (0, 0))
        tail_spec = pl.BlockSpec((tm, 2 * d_ff), lambda i: (0, 0))
        tail_shape = jax.ShapeDtypeStruct((tm, 2 * d_ff), F32)
    else:
        assert seq_rows % tm == 0
        per_seq = seq_rows // tm
        init_spec = pl.BlockSpec((None, CONV_W - 1, 2 * d_ff), lambda i: (i // per_seq, 0, 0))
        tail_spec = init_spec
        tail_shape = jax.ShapeDtypeStruct(init.shape, F32)
    row_spec = lambda w: pl.BlockSpec((tm, w), lambda i: (i, 0))
    return pl.pallas_call(
        functools.partial(_ffn_body, seq_rows=seq_rows), grid=(t // tm,),
        in_specs=([row_spec(d), row_spec(WIDTH_A), row_spec(WIDTH_B)]
                  + [_layer_spec(a, layer) for a in (ng, wo, wu, wc, bc, wd)] + [init_spec]),
        out_specs=[row_spec(d), tail_spec],
        out_shape=[jax.ShapeDtypeStruct((t, d), F32), tail_shape],
        scratch_shapes=[pltpu.VMEM((SUBLANES, 2 * d_ff), F32), pltpu.VMEM((tm, d_ff), BF16)],
        compiler_params=_params(1), name="ffn",
    )(x2d, att, ml, ng, wo, wu, wc, bc, wd, init)


def _bias_rows(rel_table_l):
    n_far = ATT_WINDOW - REL_CLIP + 1
    far = jnp.broadcast_to(rel_table_l[:, 2 * REL_CLIP:], (N_HEADS_A, n_far))
    lo = ATT_WINDOW + REL_CLIP - (ATT_BAND - 1)
    near = rel_table_l[:, lo:2 * REL_CLIP][:, ::-1]
    wrap = jnp.broadcast_to(rel_table_l[:, 2 * REL_CLIP:], (N_HEADS_A, BIAS_W - ATT_BAND))
    return jnp.concatenate([far, near, wrap], axis=1)


def _pad_seq(a, seq_pad):
    return a if a.shape[1] == seq_pad else jnp.pad(a, ((0, 0), (0, seq_pad - a.shape[1]), (0, 0)))


def _pad_cols(a, nb, seq, seq_pad):
    if seq == seq_pad:
        return a
    return jnp.pad(a.reshape(-1, nb, seq), ((0, 0), (0, 0), (0, seq_pad - seq))).reshape(a.shape[0], -1)


def _layer(x, k_cache, v_cache, c0, n0, m0, conv_buf, w):
    nb, seq, d = x.shape
    x2d = x.reshape(nb * seq, d)
    blk = MLSTM_BLOCK if seq % MLSTM_BLOCK == 0 else LANES
    seg = min(blk, seq)
    l = w["layer"]
    qa, va, kb, ft, gcb, gcx = _inproj(x2d, l, w["norm_g"], w["w_main"], w["w_ft"], w["gate_bias"], seg)

    keep = min(ATT_WINDOW, seq) if k_cache is None else seq
    k_tail = jnp.stack([ft[:WIDTH_A, (b + 1) * seq - keep:(b + 1) * seq] for b in range(nb)])
    if k_cache is None:
        att = _attn_prompt(qa, ft, va, l, w["gext"], w["g_att"], nb)
    else:
        att = _attn_step(qa, k_tail, va, k_cache, v_cache, l, w["gext"], w["g_att"])
    new_k = k_tail.reshape(nb, N_HEADS_A, HEAD_DIM_A, keep).astype(F32).swapaxes(-1, -2)
    new_v = va.reshape(nb, seq, WIDTH_A)[:, seq - keep:].reshape(nb, keep, N_HEADS_A, HEAD_DIM_A)
    new_v = new_v.transpose(0, 2, 1, 3).astype(F32)

    seq_pad = -(-seq // blk) * blk
    m0b = jnp.broadcast_to(m0.reshape(-1, 1), (nb * N_HEADS_B, LANES))
    r3 = lambda a: _pad_seq(a.reshape(nb, seq, -1), seq_pad)
    hb, c1, n1, m_b = _mlstm(r3(kb), _pad_cols(ft, nb, seq, seq_pad), _pad_cols(gcb, nb, seq, seq_pad), r3(gcx),
                             l, w["g_mlstm"], c0, n0[:, :, None, :], m0b, blk, seq - (seq_pad - blk))
    hb = hb[:, :seq].reshape(nb * seq, WIDTH_B)
    n1, m1 = n1[:, :, 0, :], m_b[:, 0].reshape(nb, N_HEADS_B)

    tm = min(FFN_ROWS, nb * seq)
    if seq < tm:
        init = jnp.pad(conv_buf, ((0, 0), (0, seq - (CONV_W - 1)), (0, 0))).reshape(nb * seq, -1)
        y, u = _ffn(x2d, att, hb, l, w["norm_g"], w["w_out"], w["w_up"], w["w_conv"], w["b_conv"],
                    w["w_down"], init, seq)
        new_buf = u.reshape(nb, seq, -1)[:, seq - (CONV_W - 1):]
    else:
        y, new_buf = _ffn(x2d, att, hb, l, w["norm_g"], w["w_out"], w["w_up"], w["w_conv"], w["b_conv"],
                          w["w_down"], conv_buf, seq)
    return y.reshape(nb, seq, d), new_k, new_v, c1, n1, m1, new_buf


def _stacked_weights(norm_g, w_in, b_i, b_f, rel_table, g_att, g_mlstm, w_out, w_up, w_conv, b_conv, w_down):
    wa, wb = WIDTH_A, WIDTH_B
    assert wa == wb
    col = lambda k: slice(k * wa, (k + 1) * wa)
    wi = w_in.astype(BF16)
    w_main = jnp.concatenate([wi[..., col(k)] for k in (0, 2, 4)], axis=-1)
    wg = w_in[..., 7 * wa:]
    wg_hi = wg.astype(BF16)
    wg_lo = (wg - wg_hi.astype(F32)).astype(BF16)
    w_ft = jnp.concatenate([wi[..., col(k)] for k in (1, 3, 5, 6)] + [wg_hi, wg_lo], axis=-1).swapaxes(1, 2)
    return dict(
        norm_g=norm_g, w_main=w_main, w_ft=w_ft, gate_bias=jnp.concatenate([b_i, b_f], axis=-1)[..., None],
        gext=jax.vmap(_bias_rows)(rel_table), g_att=g_att[:, None, :], g_mlstm=g_mlstm[:, None, :],
        w_out=w_out.astype(BF16), w_up=w_up.astype(BF16), w_conv=w_conv, b_conv=b_conv[:, None, :],
        w_down=w_down.astype(BF16))


def kernel(x_prompt, x_sample, cache_k_att, cache_v_att, state_mlstm_c, state_mlstm_n, state_mlstm_m,
           cache_ffn_conv, norm_g, w_in, b_i, b_f, rel_table, g_att, g_mlstm, w_out, w_up, w_conv,
           b_conv, w_down):
    depth = w_in.shape[0]
    bp = x_prompt.shape[0]
    d_ff2 = w_up.shape[2]
    dh = HEAD_DIM_B
    c_zero = jnp.zeros((bp, N_HEADS_B, dh, dh), F32)
    n_zero = jnp.zeros((bp, N_HEADS_B, dh), F32)
    m_zero = jnp.zeros((bp, N_HEADS_B), F32)
    buf_zero = jnp.zeros((bp, CONV_W - 1, d_ff2), F32)
    xp, xs = x_prompt, x_sample
    ck_t = cache_k_att.swapaxes(-1, -2)
    cv_t = cache_v_att.swapaxes(-1, -2)
    outs_p, outs_s = [], []
    stacks = _stacked_weights(norm_g, w_in, b_i, b_f, rel_table, g_att, g_mlstm, w_out, w_up, w_conv, b_conv,
                              w_down)
    for l in range(depth):
        w = dict(stacks, layer=l)
        xp, *st_p = _layer(xp, None, None, c_zero, n_zero, m_zero, buf_zero, w)
        xs, *st_s = _layer(xs, ck_t, cv_t, state_mlstm_c[l], state_mlstm_n[l],
                           state_mlstm_m[l], cache_ffn_conv[l], w)
        outs_p.append(st_p)
        outs_s.append(st_s)
    stack = lambda outs, i: jnp.stack([o[i] for o in outs])
    return (xp, xs) + tuple(stack(outs_p, i) for i in range(6)) + tuple(stack(outs_s, i) for i in range(6))
```

```python
import functools
import math

import jax
import jax.numpy as jnp
from jax import lax
from jax.experimental import pallas as pl
from jax.experimental.pallas import tpu as pltpu

F32 = jnp.float32
BF16 = jnp.bfloat16

CHUNK = 64
LEFT_CHUNKS = 8
ATT_WINDOW = LEFT_CHUNKS * CHUNK
REL_CLIP = 256
N_HEADS_A = 8
HEAD_DIM_A = 64
WIDTH_A = N_HEADS_A * HEAD_DIM_A
N_HEADS_B = 4
HEAD_DIM_B = 128
WIDTH_B = N_HEADS_B * HEAD_DIM_B
CONV_W = 3
EPS = 1e-6
NEG = -1e30
LOG2E = math.log2(math.e)

LANES = 128
SUBLANES = 8
VMEM_LIMIT_BYTES = 56 * 1024 * 1024

INPROJ_ROWS = 1024
ATT_ROWS = ATT_WINDOW
ATT_GROUP = 4 * CHUNK
ATT_BAND = ATT_GROUP + ATT_WINDOW
BIAS_W = 1024
PAIR_W = 2 * HEAD_DIM_A
ATT_STEP_BATCHES = 2
MLSTM_BLOCK = 256
MLSTM_BATCHES = 2
MLSTM_EXT_PAD = 16
FFN_ROWS = 512
FFN_COLS = 256


def _params(n_axes):
    return pltpu.CompilerParams(dimension_semantics=("arbitrary",) * n_axes,
                                vmem_limit_bytes=VMEM_LIMIT_BYTES)


def _layer_spec(stacked, layer):
    rest = (0,) * (stacked.ndim - 1)
    return pl.BlockSpec((None,) + stacked.shape[1:], lambda *_: (layer,) + rest, pipeline_mode=pl.Buffered(1))


def _rms(x):
    return x * lax.rsqrt(jnp.mean(x * x, axis=-1, keepdims=True) + EPS)


def _dot(a, b):
    return jnp.dot(a, b, preferred_element_type=F32)


def _dot_nt(a, b):
    return lax.dot_general(a, b, (((1,), (1,)), ((), ())), preferred_element_type=F32)


def _inproj_body(x_ref, g_ref, wm_ref, wft_ref, gb_ref,
                 qa_ref, va_ref, kb_ref, ft_ref, gcb_ref, gcx_ref, *, seg):
    n_feat = ft_ref.shape[0]
    tm = x_ref.shape[0]
    hb = (_rms(x_ref[...]) * g_ref[0:1, :]).astype(BF16)
    zf = _dot_nt(wft_ref[...], hb)
    ft_ref[...] = zf[:n_feat].astype(BF16)
    pre = zf[n_feat:n_feat + 2 * N_HEADS_B] + zf[n_feat + 2 * N_HEADS_B:] + gb_ref[...]
    lf = jnp.minimum(pre, 0.0) - jnp.log(1.0 + jnp.exp(-jnp.abs(pre)))
    pos = lax.broadcasted_iota(jnp.int32, pre.shape, 1) & (seg - 1)
    k = 1
    while k < seg:
        lf = lf + jnp.where(pos >= k, pltpu.roll(lf, k, 1), 0.0)
        k *= 2
    row = lax.broadcasted_iota(jnp.int32, pre.shape, 0)
    gcb = jnp.where(row < N_HEADS_B, pre, lf)
    gcb_ref[...] = gcb
    c = gcb[:N_HEADS_B] - gcb[N_HEADS_B:]
    gcx_ref[...] = jnp.concatenate([c, jnp.zeros((LANES - N_HEADS_B, tm), F32)], axis=0).T
    for c_idx, (o_ref, scale) in enumerate(((qa_ref, HEAD_DIM_A ** -0.5 * LOG2E), (va_ref, None),
                                            (kb_ref, HEAD_DIM_B ** -0.5))):
        z = _dot(hb, wm_ref[:, c_idx * WIDTH_A:(c_idx + 1) * WIDTH_A])
        if scale is not None:
            z = z * scale
        o_ref[...] = z.astype(BF16)


def _inproj(x2d, layer, ng, wm, wft, gb, seg):
    t, d = x2d.shape
    tm = min(INPROJ_ROWS, t)
    assert t % tm == 0 and tm % seg == 0 and seg & (seg - 1) == 0
    n_feat = wft.shape[1] - 4 * N_HEADS_B
    row_spec = lambda w: pl.BlockSpec((tm, w), lambda i: (i, 0))
    col_spec = lambda r: pl.BlockSpec((r, tm), lambda i: (0, i))
    out_shape = [jax.ShapeDtypeStruct((t, WIDTH_A), BF16)] * 3 + [
        jax.ShapeDtypeStruct((n_feat, t), BF16), jax.ShapeDtypeStruct((2 * N_HEADS_B, t), F32),
        jax.ShapeDtypeStruct((t, LANES), F32)]
    out_specs = [row_spec(WIDTH_A)] * 3 + [col_spec(n_feat), col_spec(2 * N_HEADS_B), row_spec(LANES)]
    return pl.pallas_call(
        functools.partial(_inproj_body, seg=seg), grid=(t // tm,),
        in_specs=[row_spec(d)] + [_layer_spec(a, layer) for a in (ng, wm, wft, gb)],
        out_specs=out_specs, out_shape=out_shape, compiler_params=_params(1), name="inproj",
    )(x2d, ng, wm, wft, gb)


def _mlstm_body(*refs, n_valid, nbs):
    k_ref, gcx_ref = refs[:2]
    ft_refs = refs[2:2 + 3 * nbs]
    gcb_refs = refs[2 + 3 * nbs:2 + 4 * nbs]
    (g_ref, c0_ref, n0_ref, m0_ref, h_ref, c_out_ref, n_out_ref, m_out_ref, ct_sc, m_sc) = refs[2 + 4 * nbs:]
    j = pl.program_id(1)
    blk = k_ref.shape[1]
    dh = HEAD_DIM_B
    ext_rows = ct_sc.shape[2]

    @pl.when(j == 0)
    def _load_state():
        m_sc[...] = m0_ref[...]
        for bi in range(nbs):
            for hd in range(N_HEADS_B):
                ct_sc[bi, hd, :dh, :] = c0_ref[bi, hd].T
                ct_sc[bi, hd, dh:, :] = jnp.concatenate([n0_ref[bi, hd], jnp.zeros((ext_rows - dh - 1, dh), F32)], axis=0)

    row = lax.broadcasted_iota(jnp.int32, (blk, blk), 0)
    col = lax.broadcasted_iota(jnp.int32, (blk, blk), 1)
    valid = row <= col
    if n_valid < blk:
        valid = valid & (row < n_valid)
    lane1 = lax.broadcasted_iota(jnp.int32, (1, blk), 1)
    r_tail = lax.broadcasted_iota(jnp.int32, (ext_rows - dh, blk), 0)
    ones_tail = jnp.where(r_tail == 0, 1.0, 0.0).astype(BF16)
    last = n_valid - 1
    def weights(bi, hd):
        s_idx = bi * N_HEADS_B + hd
        ig_r = gcb_refs[bi][hd:hd + 1, :]
        b_r = gcb_refs[bi][N_HEADS_B + hd:N_HEADS_B + hd + 1, :]
        m_prev = m_sc[s_idx:s_idx + 1, 0:1]
        dmat = jnp.where(valid, gcx_ref[bi, :, hd:hd + 1] + b_r, NEG)
        inter = b_r + m_prev
        m_t = jnp.maximum(inter, jnp.max(dmat, axis=0, keepdims=True))
        w_intra = jnp.exp(dmat - m_t)
        w_inter = jnp.exp(inter - m_t)
        m_new = m_t[:, last:last + 1]
        b_last = b_r[:, last:last + 1]
        w_s = jnp.exp(b_last - b_r + ig_r - m_new)
        if n_valid < blk:
            w_s = jnp.where(lane1 < n_valid, w_s, 0.0)
        decay = jnp.exp(b_last + m_prev - m_new)
        m_sc[s_idx:s_idx + 1, :] = jnp.broadcast_to(m_new, (1, m_sc.shape[1]))
        return m_t, w_intra, w_inter, w_s, decay

    def mix(bi, hd, m_t, w_intra, w_inter, w_s, decay):
        qt_ref, vt_ref, ot_ref = ft_refs[3 * bi:3 * bi + 3]
        sl = slice(hd * dh, (hd + 1) * dh)
        qt = qt_ref[sl, :]
        kk = k_ref[bi, :, sl]
        ct_old = ct_sc[bi, hd]
        kc = _dot(jnp.concatenate([kk, ct_old.astype(BF16)], axis=0), qt)
        s_t = (kc[:blk] * w_intra).astype(BF16)
        vext = jnp.concatenate([vt_ref[sl, :], ones_tail], axis=0)
        ext = _dot(vext, s_t) + w_inter * kc[blk:]
        hh = ext[:dh] / jnp.maximum(jnp.abs(ext[dh:dh + 1]), jnp.exp(-m_t))
        hh = hh * lax.rsqrt(jnp.mean(hh * hh, axis=0, keepdims=True) + EPS)
        gate = 1.0 / (1.0 + jnp.exp(-ot_ref[sl, :].astype(F32)))
        h_ref[bi, :, sl] = ((hh * gate).T * g_ref[:, sl]).astype(BF16)
        vtw = (vext.astype(F32) * w_s).astype(BF16)
        ct_sc[bi, hd] = decay * ct_old + _dot(vtw, kk)

    items = [(bi, hd) for bi in range(nbs) for hd in range(N_HEADS_B)]
    w_next = weights(*items[0])
    for k, item in enumerate(items):
        w_now = w_next
        if k + 1 < len(items):
            w_next = weights(*items[k + 1])
        mix(*item, *w_now)

    @pl.when(j == pl.num_programs(1) - 1)
    def _store_state():
        m_out_ref[...] = m_sc[...]
        for bi in range(nbs):
            for hd in range(N_HEADS_B):
                c_out_ref[bi, hd] = ct_sc[bi, hd, :dh, :].T
                n_out_ref[bi, hd] = ct_sc[bi, hd, dh:dh + 1, :]


def _mlstm(k, ft, gcb, gcx, layer, g, c0, n0, m0, blk, n_valid):
    nb, seq, _ = k.shape
    nblk = seq // blk
    nbs = math.gcd(nb, MLSTM_BATCHES)
    dh = HEAD_DIM_B
    tok = lambda w: pl.BlockSpec((nbs, blk, w), lambda bg, j: (bg, j, 0))
    col_blk = lambda k_: (lambda bg, j: (bg * nbs + k_) * nblk + j)
    ft_specs = [pl.BlockSpec((WIDTH_B, blk), lambda bg, j, r=r, f=col_blk(k_): (r, f(bg, j)))
                for k_ in range(nbs) for r in (1, 2, 3)]
    gcb_specs = [pl.BlockSpec((2 * N_HEADS_B, blk), lambda bg, j, f=col_blk(k_): (0, f(bg, j))) for k_ in range(nbs)]
    cspec = pl.BlockSpec((nbs, N_HEADS_B, dh, dh), lambda bg, j: (bg, 0, 0, 0))
    nspec = pl.BlockSpec((nbs, N_HEADS_B, 1, dh), lambda bg, j: (bg, 0, 0, 0))
    mspec = pl.BlockSpec((nbs * N_HEADS_B, LANES), lambda bg, j: (bg, 0))
    return pl.pallas_call(
        functools.partial(_mlstm_body, n_valid=n_valid, nbs=nbs), grid=(nb // nbs, nblk),
        in_specs=[tok(WIDTH_B), tok(LANES)] + ft_specs + gcb_specs + [_layer_spec(g, layer), cspec, nspec, mspec],
        out_specs=[tok(WIDTH_B), cspec, nspec, mspec],
        out_shape=[jax.ShapeDtypeStruct(k.shape, BF16), jax.ShapeDtypeStruct(c0.shape, F32),
                   jax.ShapeDtypeStruct(n0.shape, F32), jax.ShapeDtypeStruct(m0.shape, F32)],
        scratch_shapes=[pltpu.VMEM((nbs, N_HEADS_B, dh + (MLSTM_EXT_PAD if n_valid == blk else dh), dh), F32),
                        pltpu.VMEM((nbs * N_HEADS_B, LANES), F32)],
        compiler_params=_params(2), name="mlstm",
    )(k, gcx, *([ft] * (3 * nbs)), *([gcb] * nbs), g, c0, n0, m0)


def _toeplitz_bias(gext_row, rows):
    gx = jnp.broadcast_to(gext_row, (rows, BIAS_W))
    return pltpu.roll(gx, 0, 1, stride=1, stride_axis=0)


def _attn_prompt_body(q_ref, kp_ref, kc_ref, vp_ref, vc_ref, gext_ref, gatt_ref, *rest):
    n_cast = (len(rest) - 3) // 2
    cast_in, o_ref, cast_out, (bias_sc, o_sc) = rest[:n_cast], rest[n_cast], rest[n_cast + 1:-2], rest[-2:]
    b = pl.program_id(0)
    i = pl.program_id(1)
    n_pairs = N_HEADS_A // 2
    for src, dst in zip(cast_in, cast_out):
        dst[...] = src[...].astype(BF16)

    @pl.when((b == 0) & (i == 0))
    def _init_bias():
        r = lax.broadcasted_iota(jnp.int32, (ATT_GROUP, BIAS_W), 0)
        c = lax.broadcasted_iota(jnp.int32, (ATT_GROUP, BIAS_W), 1)
        first = r - (r & (CHUNK - 1))
        ok = (c >= first) & (c < first + ATT_WINDOW + CHUNK)
        for hd in range(N_HEADS_A):
            t = _toeplitz_bias(gext_ref[hd:hd + 1, :], ATT_GROUP) * LOG2E
            bias_sc[hd // 2, (hd % 2) * ATT_GROUP:(hd % 2 + 1) * ATT_GROUP, :] = jnp.where(ok, t, NEG)[:, :ATT_BAND]

    lane = lax.broadcasted_iota(jnp.int32, (ATT_GROUP, PAIR_W), 1)
    low = lane < HEAD_DIM_A

    def step(has_prev):
        def logits(g, p):
            r0 = g * ATT_GROUP
            n_cur = r0 + ATT_GROUP
            ps = slice(p * PAIR_W, (p + 1) * PAIR_W)
            qp = q_ref[r0:r0 + ATT_GROUP, ps]
            zero = jnp.zeros_like(qp)
            qq = jnp.concatenate([jnp.where(low, qp, zero), jnp.where(low, zero, qp)], axis=0)
            s_cur = _dot(qq, kc_ref[ps, :n_cur])
            if has_prev:
                return jnp.concatenate([_dot(qq, kp_ref[ps, r0:]), s_cur], axis=1) + bias_sc[p]
            return s_cur + bias_sc[p, :, ATT_BAND - n_cur:]

        def softmax(s):
            e = jnp.exp2(s - jnp.max(s, axis=-1, keepdims=True))
            return e.astype(BF16), jnp.sum(e, axis=-1, keepdims=True)

        def values(g, p, eb, l):
            r0 = g * ATT_GROUP
            n_prev = ATT_WINDOW - r0 if has_prev else 0
            n_cur = r0 + ATT_GROUP
            ps = slice(p * PAIR_W, (p + 1) * PAIR_W)
            o = _dot(eb[:, n_prev:], vc_ref[:n_cur, ps])
            if has_prev:
                o = o + _dot(eb[:, :n_prev], vp_ref[r0:, ps])
            o = o / l
            o_sc[g, :, ps] = jnp.where(low, o[:ATT_GROUP], o[ATT_GROUP:])
            if p == n_pairs - 1:
                o_ref[r0:r0 + ATT_GROUP, :] = (_rms(o_sc[g]) * gatt_ref[...]).astype(BF16)

        items = [(g, p) for g in range(ATT_ROWS // ATT_GROUP) for p in range(n_pairs)]
        s_next = logits(*items[0])
        for k, item in enumerate(items):
            s_now = s_next
            if k + 1 < len(items):
                s_next = logits(*items[k + 1])
            values(*item, *softmax(s_now))

    @pl.when(i == 0)
    def _first_block():
        step(False)

    @pl.when(i > 0)
    def _later_block():
        step(True)


def _cast_specs(stack, layer, nb, nblk):
    _, rows, cols = stack.shape
    share = 1
    while (rows * share) % (nb * nblk) or (rows * share // (nb * nblk)) % (2 * SUBLANES):
        share *= 2
    r = rows * share // (nb * nblk)
    src = pl.BlockSpec((None, r, cols), lambda b, i: (layer, (b * nblk + i) // share, 0))
    dst = pl.BlockSpec((None, r, cols), lambda b, i: (0, (b * nblk + i) // share, 0))
    return src, dst, jax.ShapeDtypeStruct((1, rows, cols), BF16)


def _attn_prompt(qa, kt, va, layer, gext, gatt, nb, cast):
    t = qa.shape[0]
    nblk = t // (nb * ATT_ROWS)
    cast_specs = [_cast_specs(a, layer, nb, nblk) for a in cast]
    cur = pl.BlockSpec((ATT_ROWS, WIDTH_A), lambda b, i: (b * nblk + i, 0))
    prev = pl.BlockSpec((ATT_ROWS, WIDTH_A), lambda b, i: (b * nblk + jnp.maximum(i - 1, 0), 0))
    cur_t = pl.BlockSpec((WIDTH_A, ATT_ROWS), lambda b, i: (0, b * nblk + i))
    prev_t = pl.BlockSpec((WIDTH_A, ATT_ROWS), lambda b, i: (0, b * nblk + jnp.maximum(i - 1, 0)))
    return pl.pallas_call(
        _attn_prompt_body, grid=(nb, nblk),
        in_specs=([cur, prev_t, cur_t, prev, cur, _layer_spec(gext, layer), _layer_spec(gatt, layer)]
                  + [c[0] for c in cast_specs]),
        out_specs=[cur] + [c[1] for c in cast_specs],
        out_shape=[jax.ShapeDtypeStruct((t, WIDTH_A), BF16)] + [c[2] for c in cast_specs],
        scratch_shapes=[pltpu.VMEM((N_HEADS_A // 2, 2 * ATT_GROUP, ATT_BAND), F32),
                        pltpu.VMEM((ATT_ROWS // ATT_GROUP, ATT_GROUP, WIDTH_A), F32)],
        compiler_params=_params(2), name="attn_prompt",
    )(qa, kt, kt, va, va, gext, gatt, *cast)


def _attn_step_body(q_ref, kn_ref, vn_ref, ck_ref, cv_ref, gext_ref, gatt_ref, o_ref, o_sc):
    nbs, _, s_new = kn_ref.shape
    l_cache = ck_ref.shape[3]
    for hd in range(N_HEADS_A):
        sl = slice(hd * HEAD_DIM_A, (hd + 1) * HEAD_DIM_A)
        bias = _toeplitz_bias(gext_ref[hd:hd + 1, :], s_new) * LOG2E
        for bi in range(nbs):
            rows = slice(bi * s_new, (bi + 1) * s_new)
            q = q_ref[rows, sl]
            s1 = _dot(q, ck_ref[bi, hd].astype(BF16)) + bias[:, :l_cache]
            s2 = _dot(q, kn_ref[bi, sl, :]) + bias[:, l_cache:l_cache + s_new]
            m = jnp.maximum(jnp.max(s1, axis=-1, keepdims=True), jnp.max(s2, axis=-1, keepdims=True))
            p1 = jnp.exp2(s1 - m)
            p2 = jnp.exp2(s2 - m)
            l = jnp.sum(p1, axis=-1, keepdims=True) + jnp.sum(p2, axis=-1, keepdims=True)
            o = _dot_nt(p1.astype(BF16), cv_ref[bi, hd].astype(BF16)) + _dot(p2.astype(BF16), vn_ref[rows, sl])
            o_sc[rows, sl] = o / l
    o_ref[...] = (_rms(o_sc[...]) * gatt_ref[...]).astype(BF16)


def _attn_step(qa, kt_new, va, ck_t, cv_t, layer, gext, gatt):
    _, nb, nh, dh, l_cache = ck_t.shape
    s_new = qa.shape[0] // nb
    nbs = math.gcd(nb, ATT_STEP_BATCHES)
    tok = pl.BlockSpec((nbs * s_new, WIDTH_A), lambda b: (b, 0))
    cache = pl.BlockSpec((None, nbs, nh, dh, l_cache), lambda b: (layer, b, 0, 0, 0))
    return pl.pallas_call(
        _attn_step_body, grid=(nb // nbs,),
        in_specs=[tok, pl.BlockSpec((nbs, WIDTH_A, s_new), lambda b: (b, 0, 0)), tok, cache, cache,
                  _layer_spec(gext, layer), _layer_spec(gatt, layer)],
        out_specs=tok, out_shape=jax.ShapeDtypeStruct(qa.shape, BF16),
        scratch_shapes=[pltpu.VMEM((nbs * s_new, WIDTH_A), F32)],
        compiler_params=_params(1), name="attn_step",
    )(qa, kt_new, va, ck_t, cv_t, gext, gatt)


GELU_C = math.sqrt(2.0 / math.pi)
GELU_A = 0.044715


def _gelu_gate(x, up):
    e = jnp.exp2(x * ((-2.0 * GELU_C * GELU_A * LOG2E) * (x * x) + (-2.0 * GELU_C * LOG2E)))
    return (x * up) / (1.0 + e)


def _shift_rows(u, prev_tile, k):
    tm, w = u.shape
    u3 = jnp.concatenate([prev_tile[None], u.reshape(tm // SUBLANES, SUBLANES, w)], axis=0)
    r = pltpu.roll(u3, k, 1)
    sub = lax.broadcasted_iota(jnp.int32, (1, SUBLANES, w), 1)
    return jnp.where(sub < k, r[:-1], r[1:]).reshape(tm, w)


def _ffn_body(x_ref, att_ref, ml_ref, ng_ref, wo_ref, wu_ref, wc_ref, bc_ref, wd_ref, init_ref,
              y_ref, tail_ref, carry_sc, a_sc, *, seq_rows):
    tm = x_ref.shape[0]
    d_ff = wd_ref.shape[0]
    multi_seq = seq_rows < tm

    if not multi_seq:
        @pl.when(pl.program_id(0) % (seq_rows // tm) == 0)
        def _seq_start():
            carry_sc[:SUBLANES - (CONV_W - 1), :] = jnp.zeros((SUBLANES - (CONV_W - 1), carry_sc.shape[1]), F32)
            carry_sc[SUBLANES - (CONV_W - 1):, :] = init_ref[...]

    mix = _dot(att_ref[...], wo_ref[:WIDTH_A, :]) + _dot(ml_ref[...], wo_ref[WIDTH_A:, :])
    x1 = x_ref[...] + _rms(mix) * ng_ref[1:2, :]
    h2 = (_rms(x1) * ng_ref[2:3, :]).astype(BF16)
    if multi_seq:
        pos = lax.broadcasted_iota(jnp.int32, (tm, FFN_COLS), 0) & (seq_rows - 1)

    def conv(cols):
        u = _dot(h2, wu_ref[:, cols])
        if multi_seq:
            fix = init_ref[:, cols]
            um2 = jnp.where(pos < 2, fix, pltpu.roll(u, 2, 0))
            um1 = jnp.where(pos < 1, pltpu.roll(fix, tm - 1, 0), pltpu.roll(u, 1, 0))
            tail_ref[:, cols] = u
        else:
            prev = carry_sc[:, cols]
            um2 = _shift_rows(u, prev, 2)
            um1 = _shift_rows(u, prev, 1)
            carry_sc[:, cols] = u[tm - SUBLANES:, :]
            tail_ref[:, cols] = u[tm - (CONV_W - 1):, :]
        return bc_ref[:, cols] + um2 * wc_ref[0:1, cols] + um1 * wc_ref[1:2, cols] + u * wc_ref[2:3, cols]

    for c in range(d_ff // FFN_COLS):
        gate = conv(slice(c * FFN_COLS, (c + 1) * FFN_COLS))
        up = conv(slice(d_ff + c * FFN_COLS, d_ff + (c + 1) * FFN_COLS))
        a_sc[:, c * FFN_COLS:(c + 1) * FFN_COLS] = _gelu_gate(gate, up).astype(BF16)
    f = _dot(a_sc[...], wd_ref[...])
    y_ref[...] = x1 + _rms(f) * ng_ref[3:4, :]


def _ffn(x2d, att, ml, layer, ng, wc, bc, wo, wu, wd, init, seq_rows):
    t, d = x2d.shape
    d_ff = wd.shape[1]
    tm = min(FFN_ROWS, t)
    assert t % tm == 0 and d_ff % FFN_COLS == 0
    multi_seq = seq_rows < tm
    if multi_seq:
        assert tm % seq_rows == 0 and t == tm and seq_rows & (seq_rows - 1) == 0
        init_spec = pl.BlockSpec(init.shape, lambda i: (0, 0))
        tail_spec = pl.BlockSpec((tm, 2 * d_ff), lambda i: (0, 0))
        tail_shape = jax.ShapeDtypeStruct((tm, 2 * d_ff), F32)
    else:
        assert seq_rows % tm == 0
        per_seq = seq_rows // tm
        init_spec = pl.BlockSpec((None, CONV_W - 1, 2 * d_ff), lambda i: (i // per_seq, 0, 0))
        tail_spec = init_spec
        tail_shape = jax.ShapeDtypeStruct(init.shape, F32)
    row_spec = lambda w: pl.BlockSpec((tm, w), lambda i: (i, 0))
    return pl.pallas_call(
        functools.partial(_ffn_body, seq_rows=seq_rows), grid=(t // tm,),
        in_specs=([row_spec(d), row_spec(WIDTH_A), row_spec(WIDTH_B)]
                  + [_layer_spec(ng, layer), _layer_spec(wo, 0), _layer_spec(wu, 0), _layer_spec(wc, layer),
                     _layer_spec(bc, layer), _layer_spec(wd, 0), init_spec]),
        out_specs=[row_spec(d), tail_spec],
        out_shape=[jax.ShapeDtypeStruct((t, d), F32), tail_shape],
        scratch_shapes=[pltpu.VMEM((SUBLANES, 2 * d_ff), F32), pltpu.VMEM((tm, d_ff), BF16)],
        compiler_params=_params(1), name="ffn",
    )(x2d, att, ml, ng, wo, wu, wc, bc, wd, init)


def _bias_rows(rel_table_l):
    n_far = ATT_WINDOW - REL_CLIP + 1
    far = jnp.broadcast_to(rel_table_l[:, 2 * REL_CLIP:], (N_HEADS_A, n_far))
    lo = ATT_WINDOW + REL_CLIP - (ATT_BAND - 1)
    near = rel_table_l[:, lo:2 * REL_CLIP][:, ::-1]
    wrap = jnp.broadcast_to(rel_table_l[:, 2 * REL_CLIP:], (N_HEADS_A, BIAS_W - ATT_BAND))
    return jnp.concatenate([far, near, wrap], axis=1)


def _pad_seq(a, seq_pad):
    return a if a.shape[1] == seq_pad else jnp.pad(a, ((0, 0), (0, seq_pad - a.shape[1]), (0, 0)))


def _pad_cols(a, nb, seq, seq_pad):
    if seq == seq_pad:
        return a
    return jnp.pad(a.reshape(-1, nb, seq), ((0, 0), (0, 0), (0, seq_pad - seq))).reshape(a.shape[0], -1)


def _layer(x, k_cache, v_cache, c0, n0, m0, conv_buf, w):
    nb, seq, d = x.shape
    x2d = x.reshape(nb * seq, d)
    blk = MLSTM_BLOCK if seq % MLSTM_BLOCK == 0 else LANES
    seg = min(blk, seq)
    l = w["layer"]
    qa, va, kb, ft, gcb, gcx = _inproj(x2d, l, w["norm_g"], w["w_main"], w["w_ft"], w["gate_bias"], seg)

    keep = min(ATT_WINDOW, seq) if k_cache is None else seq
    k_tail = jnp.stack([ft[:WIDTH_A, (b + 1) * seq - keep:(b + 1) * seq] for b in range(nb)])
    if k_cache is None:
        att, *w["ffn_w"] = _attn_prompt(qa, ft, va, l, w["gext"], w["g_att"], nb,
                                        (w["w_out"], w["w_up"], w["w_down"]))
    else:
        att = _attn_step(qa, k_tail, va, k_cache, v_cache, l, w["gext"], w["g_att"])
    new_k = k_tail.reshape(nb, N_HEADS_A, HEAD_DIM_A, keep).astype(F32).swapaxes(-1, -2)
    new_v = va.reshape(nb, seq, WIDTH_A)[:, seq - keep:].reshape(nb, keep, N_HEADS_A, HEAD_DIM_A)
    new_v = new_v.transpose(0, 2, 1, 3).astype(F32)

    seq_pad = -(-seq // blk) * blk
    m0b = jnp.broadcast_to(m0.reshape(-1, 1), (nb * N_HEADS_B, LANES))
    r3 = lambda a: _pad_seq(a.reshape(nb, seq, -1), seq_pad)
    hb, c1, n1, m_b = _mlstm(r3(kb), _pad_cols(ft, nb, seq, seq_pad), _pad_cols(gcb, nb, seq, seq_pad), r3(gcx),
                             l, w["g_mlstm"], c0, n0[:, :, None, :], m0b, blk, seq - (seq_pad - blk))
    hb = hb[:, :seq].reshape(nb * seq, WIDTH_B)
    n1, m1 = n1[:, :, 0, :], m_b[:, 0].reshape(nb, N_HEADS_B)

    tm = min(FFN_ROWS, nb * seq)
    if seq < tm:
        init = jnp.pad(conv_buf, ((0, 0), (0, seq - (CONV_W - 1)), (0, 0))).reshape(nb * seq, -1)
        y, u = _ffn(x2d, att, hb, l, w["norm_g"], w["w_conv"], w["b_conv"], *w["ffn_w"], init, seq)
        new_buf = u.reshape(nb, seq, -1)[:, seq - (CONV_W - 1):]
    else:
        y, new_buf = _ffn(x2d, att, hb, l, w["norm_g"], w["w_conv"], w["b_conv"], *w["ffn_w"], conv_buf, seq)
    return y.reshape(nb, seq, d), new_k, new_v, c1, n1, m1, new_buf


def _stacked_weights(norm_g, w_in, b_i, b_f, rel_table, g_att, g_mlstm, w_out, w_up, w_conv, b_conv, w_down):
    wa, wb = WIDTH_A, WIDTH_B
    assert wa == wb
    col = lambda k: slice(k * wa, (k + 1) * wa)
    wi = w_in.astype(BF16)
    w_main = jnp.concatenate([wi[..., col(k)] for k in (0, 2, 4)], axis=-1)
    wg = w_in[..., 7 * wa:]
    wg_hi = wg.astype(BF16)
    wg_lo = (wg - wg_hi.astype(F32)).astype(BF16)
    w_ft = jnp.concatenate([wi[..., col(k)] for k in (1, 3, 5, 6)] + [wg_hi, wg_lo], axis=-1).swapaxes(1, 2)
    return dict(
        norm_g=norm_g, w_main=w_main, w_ft=w_ft, gate_bias=jnp.concatenate([b_i, b_f], axis=-1)[..., None],
        gext=jax.vmap(_bias_rows)(rel_table), g_att=g_att[:, None, :], g_mlstm=g_mlstm[:, None, :],
        w_out=w_out, w_up=w_up, w_conv=w_conv, b_conv=b_conv[:, None, :], w_down=w_down)


def kernel(x_prompt, x_sample, cache_k_att, cache_v_att, state_mlstm_c, state_mlstm_n, state_mlstm_m,
           cache_ffn_conv, norm_g, w_in, b_i, b_f, rel_table, g_att, g_mlstm, w_out, w_up, w_conv,
           b_conv, w_down):
    depth = w_in.shape[0]
    bp = x_prompt.shape[0]
    d_ff2 = w_up.shape[2]
    dh = HEAD_DIM_B
    c_zero = jnp.zeros((bp, N_HEADS_B, dh, dh), F32)
    n_zero = jnp.zeros((bp, N_HEADS_B, dh), F32)
    m_zero = jnp.zeros((bp, N_HEADS_B), F32)
    buf_zero = jnp.zeros((bp, CONV_W - 1, d_ff2), F32)
    xp, xs = x_prompt, x_sample
    ck_t = cache_k_att.swapaxes(-1, -2)
    cv_t = cache_v_att.swapaxes(-1, -2)
    outs_p, outs_s = [], []
    stacks = _stacked_weights(norm_g, w_in, b_i, b_f, rel_table, g_att, g_mlstm, w_out, w_up, w_conv, b_conv,
                              w_down)
    for l in range(depth):
        w = dict(stacks, layer=l)
        xp, *st_p = _layer(xp, None, None, c_zero, n_zero, m_zero, buf_zero, w)
        xs, *st_s = _layer(xs, ck_t, cv_t, state_mlstm_c[l], state_mlstm_n[l],
                           state_mlstm_m[l], cache_ffn_conv[l], w)
        outs_p.append(st_p)
        outs_s.append(st_s)
    stack = lambda outs, i: jnp.stack([o[i] for o in outs])
    return (xp, xs) + tuple(stack(outs_p, i) for i in range(6)) + tuple(stack(outs_s, i) for i in range(6))
```

```python
import functools
import math

import jax
import jax.numpy as jnp
from jax import lax
from jax.experimental import pallas as pl
from jax.experimental.pallas import tpu as pltpu

F32 = jnp.float32
BF16 = jnp.bfloat16

CHUNK = 64
LEFT_CHUNKS = 8
ATT_WINDOW = LEFT_CHUNKS * CHUNK
REL_CLIP = 256
N_HEADS_A = 8
HEAD_DIM_A = 64
WIDTH_A = N_HEADS_A * HEAD_DIM_A
N_HEADS_B = 4
HEAD_DIM_B = 128
WIDTH_B = N_HEADS_B * HEAD_DIM_B
CONV_W = 3
EPS = 1e-6
NEG = -1e30
LOG2E = math.log2(math.e)

LANES = 128
SUBLANES = 8
VMEM_LIMIT_BYTES = 56 * 1024 * 1024

INPROJ_ROWS = 1024
ATT_ROWS = ATT_WINDOW
ATT_GROUP = 4 * CHUNK
ATT_BAND = ATT_GROUP + ATT_WINDOW
BIAS_W = 1024
PAIR_W = 2 * HEAD_DIM_A
ATT_STEP_BATCHES = 4
MLSTM_BLOCK = 256
MLSTM_BATCHES = 2
MLSTM_EXT_PAD = 16
FFN_ROWS = 1024
FFN_COLS = 256


def _params(n_axes):
    return pltpu.CompilerParams(dimension_semantics=("arbitrary",) * n_axes,
                                vmem_limit_bytes=VMEM_LIMIT_BYTES)


def _layer_spec(stacked, layer):
    rest = (0,) * (stacked.ndim - 1)
    return pl.BlockSpec((None,) + stacked.shape[1:], lambda *_: (layer,) + rest, pipeline_mode=pl.Buffered(1))


def _rms(x):
    return x * lax.rsqrt(jnp.mean(x * x, axis=-1, keepdims=True) + EPS)


def _dot(a, b):
    return jnp.dot(a, b, preferred_element_type=F32)


def _dot_nt(a, b):
    return lax.dot_general(a, b, (((1,), (1,)), ((), ())), preferred_element_type=F32)


def _inproj_body(x_ref, g_ref, wm_ref, wft_ref, gb_ref,
                 qa_ref, va_ref, kb_ref, ft_ref, gcb_ref, gcx_ref, *, seg):
    n_feat = ft_ref.shape[0]
    tm = x_ref.shape[0]
    hb = (_rms(x_ref[...]) * g_ref[0:1, :]).astype(BF16)
    zf = _dot_nt(wft_ref[...], hb)
    ft_ref[...] = zf[:n_feat].astype(BF16)
    pre = zf[n_feat:n_feat + 2 * N_HEADS_B] + zf[n_feat + 2 * N_HEADS_B:] + gb_ref[...]
    lf = jnp.minimum(pre, 0.0) - jnp.log(1.0 + jnp.exp(-jnp.abs(pre)))
    pos = lax.broadcasted_iota(jnp.int32, pre.shape, 1) & (seg - 1)
    k = 1
    while k < seg:
        lf = lf + jnp.where(pos >= k, pltpu.roll(lf, k, 1), 0.0)
        k *= 2
    row = lax.broadcasted_iota(jnp.int32, pre.shape, 0)
    gcb = jnp.where(row < N_HEADS_B, pre, lf)
    gcb_ref[...] = gcb
    c = gcb[:N_HEADS_B] - gcb[N_HEADS_B:]
    gcx_ref[...] = jnp.concatenate([c, jnp.zeros((LANES - N_HEADS_B, tm), F32)], axis=0).T
    for group, o_ref, scale in ((0, qa_ref, HEAD_DIM_A ** -0.5 * LOG2E), (2, va_ref, None),
                                (4, kb_ref, HEAD_DIM_B ** -0.5)):
        z = _dot(hb, wm_ref[:, group * WIDTH_A:(group + 1) * WIDTH_A])
        if scale is not None:
            z = z * scale
        o_ref[...] = z.astype(BF16)


def _inproj(x2d, layer, ng, wm, wft, gb, seg):
    t, d = x2d.shape
    tm = min(INPROJ_ROWS, t)
    assert t % tm == 0 and tm % seg == 0 and seg & (seg - 1) == 0
    n_feat = wft.shape[1] - 4 * N_HEADS_B
    row_spec = lambda w: pl.BlockSpec((tm, w), lambda i: (i, 0))
    col_spec = lambda r: pl.BlockSpec((r, tm), lambda i: (0, i))
    out_shape = [jax.ShapeDtypeStruct((t, WIDTH_A), BF16)] * 3 + [
        jax.ShapeDtypeStruct((n_feat, t), BF16), jax.ShapeDtypeStruct((2 * N_HEADS_B, t), F32),
        jax.ShapeDtypeStruct((t, LANES), F32)]
    out_specs = [row_spec(WIDTH_A)] * 3 + [col_spec(n_feat), col_spec(2 * N_HEADS_B), row_spec(LANES)]
    return pl.pallas_call(
        functools.partial(_inproj_body, seg=seg), grid=(t // tm,),
        in_specs=[row_spec(d)] + [_layer_spec(a, layer) for a in (ng, wm, wft, gb)],
        out_specs=out_specs, out_shape=out_shape, compiler_params=_params(1), name="inproj",
    )(x2d, ng, wm, wft, gb)


def _mlstm_body(*refs, n_valid, nbs):
    k_ref, gcx_ref = refs[:2]
    ft_refs = refs[2:2 + 3 * nbs]
    gcb_refs = refs[2 + 3 * nbs:2 + 4 * nbs]
    (g_ref, c0_ref, n0_ref, m0_ref, h_ref, c_out_ref, n_out_ref, m_out_ref, ct_sc, m_sc) = refs[2 + 4 * nbs:]
    j = pl.program_id(1)
    blk = k_ref.shape[1]
    dh = HEAD_DIM_B
    ext_rows = ct_sc.shape[2]

    @pl.when(j == 0)
    def _load_state():
        m_sc[...] = m0_ref[...]
        for bi in range(nbs):
            for hd in range(N_HEADS_B):
                ct_sc[bi, hd, :dh, :] = c0_ref[bi, hd].T
                ct_sc[bi, hd, dh:, :] = jnp.concatenate([n0_ref[bi, hd], jnp.zeros((ext_rows - dh - 1, dh), F32)], axis=0)

    row = lax.broadcasted_iota(jnp.int32, (blk, blk), 0)
    col = lax.broadcasted_iota(jnp.int32, (blk, blk), 1)
    valid = row <= col
    if n_valid < blk:
        valid = valid & (row < n_valid)
    lane1 = lax.broadcasted_iota(jnp.int32, (1, blk), 1)
    r_tail = lax.broadcasted_iota(jnp.int32, (ext_rows - dh, blk), 0)
    ones_tail = jnp.where(r_tail == 0, 1.0, 0.0).astype(BF16)
    last = n_valid - 1
    def weights(bi, hd):
        s_idx = bi * N_HEADS_B + hd
        ig_r = gcb_refs[bi][hd:hd + 1, :]
        b_r = gcb_refs[bi][N_HEADS_B + hd:N_HEADS_B + hd + 1, :]
        m_prev = m_sc[s_idx:s_idx + 1, 0:1]
        dmat = jnp.where(valid, gcx_ref[bi, :, hd:hd + 1] + b_r, NEG)
        inter = b_r + m_prev
        m_t = jnp.maximum(inter, jnp.max(dmat, axis=0, keepdims=True))
        w_intra = jnp.exp(dmat - m_t)
        w_inter = jnp.exp(inter - m_t)
        m_new = m_t[:, last:last + 1]
        b_last = b_r[:, last:last + 1]
        w_s = jnp.exp(b_last - b_r + ig_r - m_new)
        if n_valid < blk:
            w_s = jnp.where(lane1 < n_valid, w_s, 0.0)
        decay = jnp.exp(b_last + m_prev - m_new)
        m_sc[s_idx:s_idx + 1, :] = jnp.broadcast_to(m_new, (1, m_sc.shape[1]))
        return m_t, w_intra, w_inter, w_s, decay

    def mix(bi, hd, m_t, w_intra, w_inter, w_s, decay):
        qt_ref, vt_ref, ot_ref = ft_refs[3 * bi:3 * bi + 3]
        sl = slice(hd * dh, (hd + 1) * dh)
        qt = qt_ref[sl, :]
        kk = k_ref[bi, :, sl]
        ct_old = ct_sc[bi, hd]
        kc = _dot(jnp.concatenate([kk, ct_old.astype(BF16)], axis=0), qt)
        s_t = (kc[:blk] * w_intra).astype(BF16)
        vext = jnp.concatenate([vt_ref[sl, :], ones_tail], axis=0)
        ext = _dot(vext, s_t) + w_inter * kc[blk:]
        hh = ext[:dh] / jnp.maximum(jnp.abs(ext[dh:dh + 1]), jnp.exp(-m_t))
        hh = hh * lax.rsqrt(jnp.mean(hh * hh, axis=0, keepdims=True) + EPS)
        gate = 1.0 / (1.0 + jnp.exp(-ot_ref[sl, :].astype(F32)))
        h_ref[bi, :, sl] = ((hh * gate).T * g_ref[:, sl]).astype(BF16)
        vtw = (vext.astype(F32) * w_s).astype(BF16)
        ct_sc[bi, hd] = decay * ct_old + _dot(vtw, kk)

    items = [(bi, hd) for bi in range(nbs) for hd in range(N_HEADS_B)]
    w_next = weights(*items[0])
    for k, item in enumerate(items):
        w_now = w_next
        if k + 1 < len(items):
            w_next = weights(*items[k + 1])
        mix(*item, *w_now)

    @pl.when(j == pl.num_programs(1) - 1)
    def _store_state():
        m_out_ref[...] = m_sc[...]
        for bi in range(nbs):
            for hd in range(N_HEADS_B):
                c_out_ref[bi, hd] = ct_sc[bi, hd, :dh, :].T
                n_out_ref[bi, hd] = ct_sc[bi, hd, dh:dh + 1, :]


def _mlstm(k, ft, gcb, gcx, layer, g, c0, n0, m0, blk, n_valid):
    nb, seq, _ = k.shape
    nblk = seq // blk
    nbs = math.gcd(nb, MLSTM_BATCHES)
    dh = HEAD_DIM_B
    tok = lambda w: pl.BlockSpec((nbs, blk, w), lambda bg, j: (bg, j, 0))
    col_blk = lambda k_: (lambda bg, j: (bg * nbs + k_) * nblk + j)
    ft_specs = [pl.BlockSpec((WIDTH_B, blk), lambda bg, j, r=r, f=col_blk(k_): (r, f(bg, j)))
                for k_ in range(nbs) for r in (1, 2, 3)]
    gcb_specs = [pl.BlockSpec((2 * N_HEADS_B, blk), lambda bg, j, f=col_blk(k_): (0, f(bg, j))) for k_ in range(nbs)]
    cspec = pl.BlockSpec((nbs, N_HEADS_B, dh, dh), lambda bg, j: (bg, 0, 0, 0))
    nspec = pl.BlockSpec((nbs, N_HEADS_B, 1, dh), lambda bg, j: (bg, 0, 0, 0))
    mspec = pl.BlockSpec((nbs * N_HEADS_B, LANES), lambda bg, j: (bg, 0))
    return pl.pallas_call(
        functools.partial(_mlstm_body, n_valid=n_valid, nbs=nbs), grid=(nb // nbs, nblk),
        in_specs=[tok(WIDTH_B), tok(LANES)] + ft_specs + gcb_specs + [_layer_spec(g, layer), cspec, nspec, mspec],
        out_specs=[tok(WIDTH_B), cspec, nspec, mspec],
        out_shape=[jax.ShapeDtypeStruct(k.shape, BF16), jax.ShapeDtypeStruct(c0.shape, F32),
                   jax.ShapeDtypeStruct(n0.shape, F32), jax.ShapeDtypeStruct(m0.shape, F32)],
        scratch_shapes=[pltpu.VMEM((nbs, N_HEADS_B, dh + (MLSTM_EXT_PAD if n_valid == blk else dh), dh), F32),
                        pltpu.VMEM((nbs * N_HEADS_B, LANES), F32)],
        compiler_params=_params(2), name="mlstm",
    )(k, gcx, *([ft] * (3 * nbs)), *([gcb] * nbs), g, c0, n0, m0)


def _toeplitz_bias(gext_row, rows):
    gx = jnp.broadcast_to(gext_row, (rows, BIAS_W))
    return pltpu.roll(gx, 0, 1, stride=1, stride_axis=0)


def _attn_prompt_body(q_ref, kp_ref, kc_ref, vp_ref, vc_ref, gext_ref, gatt_ref, *rest):
    n_cast = (len(rest) - 3) // 2
    cast_in, o_ref, cast_out, (bias_sc, o_sc) = rest[:n_cast], rest[n_cast], rest[n_cast + 1:-2], rest[-2:]
    b = pl.program_id(0)
    i = pl.program_id(1)
    n_pairs = N_HEADS_A // 2
    for src, dst in zip(cast_in, cast_out):
        dst[...] = src[...].astype(BF16)

    @pl.when((b == 0) & (i == 0))
    def _init_bias():
        r = lax.broadcasted_iota(jnp.int32, (ATT_GROUP, BIAS_W), 0)
        c = lax.broadcasted_iota(jnp.int32, (ATT_GROUP, BIAS_W), 1)
        first = r - (r & (CHUNK - 1))
        ok = (c >= first) & (c < first + ATT_WINDOW + CHUNK)
        for hd in range(N_HEADS_A):
            t = _toeplitz_bias(gext_ref[hd:hd + 1, :], ATT_GROUP) * LOG2E
            bias_sc[hd // 2, (hd % 2) * ATT_GROUP:(hd % 2 + 1) * ATT_GROUP, :] = jnp.where(ok, t, NEG)[:, :ATT_BAND]

    lane = lax.broadcasted_iota(jnp.int32, (ATT_GROUP, PAIR_W), 1)
    low = lane < HEAD_DIM_A

    def step(has_prev):
        def logits(g, p):
            r0 = g * ATT_GROUP
            n_cur = r0 + ATT_GROUP
            ps = slice(p * PAIR_W, (p + 1) * PAIR_W)
            qp = q_ref[r0:r0 + ATT_GROUP, ps]
            zero = jnp.zeros_like(qp)
            qq = jnp.concatenate([jnp.where(low, qp, zero), jnp.where(low, zero, qp)], axis=0)
            s_cur = _dot(qq, kc_ref[ps, :n_cur])
            if has_prev:
                return jnp.concatenate([_dot(qq, kp_ref[ps, r0:]), s_cur], axis=1) + bias_sc[p]
            return s_cur + bias_sc[p, :, ATT_BAND - n_cur:]

        def softmax(s):
            e = jnp.exp2(s - jnp.max(s, axis=-1, keepdims=True))
            return e.astype(BF16), jnp.sum(e, axis=-1, keepdims=True)

        def values(g, p, eb, l):
            r0 = g * ATT_GROUP
            n_prev = ATT_WINDOW - r0 if has_prev else 0
            n_cur = r0 + ATT_GROUP
            ps = slice(p * PAIR_W, (p + 1) * PAIR_W)
            o = _dot(eb[:, n_prev:], vc_ref[:n_cur, ps])
            if has_prev:
                o = o + _dot(eb[:, :n_prev], vp_ref[r0:, ps])
            o = o / l
            o_sc[g, :, ps] = jnp.where(low, o[:ATT_GROUP], o[ATT_GROUP:])
            if p == n_pairs - 1:
                o_ref[r0:r0 + ATT_GROUP, :] = (_rms(o_sc[g]) * gatt_ref[...]).astype(BF16)

        items = [(g, p) for g in range(ATT_ROWS // ATT_GROUP) for p in range(n_pairs)]
        s_next = logits(*items[0])
        for k, item in enumerate(items):
            s_now = s_next
            if k + 1 < len(items):
                s_next = logits(*items[k + 1])
            values(*item, *softmax(s_now))

    @pl.when(i == 0)
    def _first_block():
        step(False)

    @pl.when(i > 0)
    def _later_block():
        step(True)


def _cast_specs(stack, layer, nb, nblk):
    _, rows, cols = stack.shape
    share = 1
    while (rows * share) % (nb * nblk) or (rows * share // (nb * nblk)) % (2 * SUBLANES):
        share *= 2
    r = rows * share // (nb * nblk)
    src = pl.BlockSpec((None, r, cols), lambda b, i: (layer, (b * nblk + i) // share, 0))
    dst = pl.BlockSpec((None, r, cols), lambda b, i: (0, (b * nblk + i) // share, 0))
    return src, dst, jax.ShapeDtypeStruct((1, rows, cols), BF16)


def _attn_prompt(qa, kt, va, layer, gext, gatt, nb, cast):
    t = qa.shape[0]
    nblk = t // (nb * ATT_ROWS)
    cast_specs = [_cast_specs(a, layer, nb, nblk) for a in cast]
    cur = pl.BlockSpec((ATT_ROWS, WIDTH_A), lambda b, i: (b * nblk + i, 0))
    prev = pl.BlockSpec((ATT_ROWS, WIDTH_A), lambda b, i: (b * nblk + jnp.maximum(i - 1, 0), 0))
    cur_t = pl.BlockSpec((WIDTH_A, ATT_ROWS), lambda b, i: (0, b * nblk + i))
    prev_t = pl.BlockSpec((WIDTH_A, ATT_ROWS), lambda b, i: (0, b * nblk + jnp.maximum(i - 1, 0)))
    return pl.pallas_call(
        _attn_prompt_body, grid=(nb, nblk),
        in_specs=([cur, prev_t, cur_t, prev, cur, _layer_spec(gext, layer), _layer_spec(gatt, layer)]
                  + [c[0] for c in cast_specs]),
        out_specs=[cur] + [c[1] for c in cast_specs],
        out_shape=[jax.ShapeDtypeStruct((t, WIDTH_A), BF16)] + [c[2] for c in cast_specs],
        scratch_shapes=[pltpu.VMEM((N_HEADS_A // 2, 2 * ATT_GROUP, ATT_BAND), F32),
                        pltpu.VMEM((ATT_ROWS // ATT_GROUP, ATT_GROUP, WIDTH_A), F32)],
        compiler_params=_params(2), name="attn_prompt",
    )(qa, kt, kt, va, va, gext, gatt, *cast)


def _attn_step_body(q_ref, kn_ref, vn_ref, ck_ref, cv_ref, gext_ref, gatt_ref, o_ref, o_sc):
    nbs, _, s_new = kn_ref.shape
    l_cache = ck_ref.shape[3]
    for hd in range(N_HEADS_A):
        sl = slice(hd * HEAD_DIM_A, (hd + 1) * HEAD_DIM_A)
        bias = _toeplitz_bias(gext_ref[hd:hd + 1, :], s_new) * LOG2E
        for bi in range(nbs):
            rows = slice(bi * s_new, (bi + 1) * s_new)
            q = q_ref[rows, sl]
            s1 = _dot(q, ck_ref[bi, hd].astype(BF16)) + bias[:, :l_cache]
            s2 = _dot(q, kn_ref[bi, sl, :]) + bias[:, l_cache:l_cache + s_new]
            m = jnp.maximum(jnp.max(s1, axis=-1, keepdims=True), jnp.max(s2, axis=-1, keepdims=True))
            p1 = jnp.exp2(s1 - m)
            p2 = jnp.exp2(s2 - m)
            l = jnp.sum(p1, axis=-1, keepdims=True) + jnp.sum(p2, axis=-1, keepdims=True)
            o = _dot_nt(p1.astype(BF16), cv_ref[bi, hd].astype(BF16)) + _dot(p2.astype(BF16), vn_ref[rows, sl])
            o_sc[rows, sl] = o / l
    o_ref[...] = (_rms(o_sc[...]) * gatt_ref[...]).astype(BF16)


def _attn_step(qa, kt_new, va, ck_t, cv_t, layer, gext, gatt):
    _, nb, nh, dh, l_cache = ck_t.shape
    s_new = qa.shape[0] // nb
    nbs = math.gcd(nb, ATT_STEP_BATCHES)
    tok = pl.BlockSpec((nbs * s_new, WIDTH_A), lambda b: (b, 0))
    cache = pl.BlockSpec((None, nbs, nh, dh, l_cache), lambda b: (layer, b, 0, 0, 0))
    return pl.pallas_call(
        _attn_step_body, grid=(nb // nbs,),
        in_specs=[tok, pl.BlockSpec((nbs, WIDTH_A, s_new), lambda b: (b, 0, 0)), tok, cache, cache,
                  _layer_spec(gext, layer), _layer_spec(gatt, layer)],
        out_specs=tok, out_shape=jax.ShapeDtypeStruct(qa.shape, BF16),
        scratch_shapes=[pltpu.VMEM((nbs * s_new, WIDTH_A), F32)],
        compiler_params=_params(1), name="attn_step",
    )(qa, kt_new, va, ck_t, cv_t, gext, gatt)


GELU_C = math.sqrt(2.0 / math.pi)
GELU_A = 0.044715


def _gelu_gate(x, up):
    e = jnp.exp2(x * ((-2.0 * GELU_C * GELU_A * LOG2E) * (x * x) + (-2.0 * GELU_C * LOG2E)))
    return (x * up) / (1.0 + e)


def _shift_rows(u, prev_tile, k):
    tm, w = u.shape
    u3 = jnp.concatenate([prev_tile[None], u.reshape(tm // SUBLANES, SUBLANES, w)], axis=0)
    r = pltpu.roll(u3, k, 1)
    sub = lax.broadcasted_iota(jnp.int32, (1, SUBLANES, w), 1)
    return jnp.where(sub < k, r[:-1], r[1:]).reshape(tm, w)


def _ffn_body(x_ref, att_ref, ml_ref, ng_ref, wo_ref, wu_ref, wc_ref, bc_ref, wd_ref, init_ref,
              y_ref, tail_ref, carry_sc, a_sc, *, seq_rows):
    tm = x_ref.shape[0]
    d_ff = wd_ref.shape[0]
    multi_seq = seq_rows < tm

    if not multi_seq:
        @pl.when(pl.program_id(0) % (seq_rows // tm) == 0)
        def _seq_start():
            carry_sc[:SUBLANES - (CONV_W - 1), :] = jnp.zeros((SUBLANES - (CONV_W - 1), carry_sc.shape[1]), F32)
            carry_sc[SUBLANES - (CONV_W - 1):, :] = init_ref[...]

    mix = _dot(att_ref[...], wo_ref[:WIDTH_A, :]) + _dot(ml_ref[...], wo_ref[WIDTH_A:, :])
    x1 = x_ref[...] + _rms(mix) * ng_ref[1:2, :]
    h2 = (_rms(x1) * ng_ref[2:3, :]).astype(BF16)
    if multi_seq:
        pos = lax.broadcasted_iota(jnp.int32, (tm, FFN_COLS), 0) & (seq_rows - 1)

    def conv(cols):
        u = _dot(h2, wu_ref[:, cols])
        if multi_seq:
            fix = init_ref[:, cols]
            um2 = jnp.where(pos < 2, fix, pltpu.roll(u, 2, 0))
            um1 = jnp.where(pos < 1, pltpu.roll(fix, tm - 1, 0), pltpu.roll(u, 1, 0))
            tail_ref[:, cols] = u
        else:
            prev = carry_sc[:, cols]
            um2 = _shift_rows(u, prev, 2)
            um1 = _shift_rows(u, prev, 1)
            carry_sc[:, cols] = u[tm - SUBLANES:, :]
            tail_ref[:, cols] = u[tm - (CONV_W - 1):, :]
        return bc_ref[:, cols] + um2 * wc_ref[0:1, cols] + um1 * wc_ref[1:2, cols] + u * wc_ref[2:3, cols]

    for c in range(d_ff // FFN_COLS):
        gate = conv(slice(c * FFN_COLS, (c + 1) * FFN_COLS))
        up = conv(slice(d_ff + c * FFN_COLS, d_ff + (c + 1) * FFN_COLS))
        a_sc[:, c * FFN_COLS:(c + 1) * FFN_COLS] = _gelu_gate(gate, up).astype(BF16)
    f = _dot(a_sc[...], wd_ref[...])
    y_ref[...] = x1 + _rms(f) * ng_ref[3:4, :]


def _ffn(x2d, att, ml, layer, ng, wc, bc, wo, wu, wd, init, seq_rows):
    t, d = x2d.shape
    d_ff = wd.shape[1]
    tm = min(FFN_ROWS, t)
    assert t % tm == 0 and d_ff % FFN_COLS == 0
    multi_seq = seq_rows < tm
    if multi_seq:
        assert tm % seq_rows == 0 and t == tm and seq_rows & (seq_rows - 1) == 0
        init_spec = pl.BlockSpec(init.shape, lambda i: (0, 0))
        tail_spec = pl.BlockSpec((tm, 2 * d_ff), lambda i: (0, 0))
        tail_shape = jax.ShapeDtypeStruct((tm, 2 * d_ff), F32)
    else:
        assert seq_rows % tm == 0
        per_seq = seq_rows // tm
        init_spec = pl.BlockSpec((None, CONV_W - 1, 2 * d_ff), lambda i: (i // per_seq, 0, 0))
        tail_spec = init_spec
        tail_shape = jax.ShapeDtypeStruct(init.shape, F32)
    row_spec = lambda w: pl.BlockSpec((tm, w), lambda i: (i, 0))
    return pl.pallas_call(
        functools.partial(_ffn_body, seq_rows=seq_rows), grid=(t // tm,),
        in_specs=([row_spec(d), row_spec(WIDTH_A), row_spec(WIDTH_B)]
                  + [_layer_spec(ng, layer), _layer_spec(wo, 0), _layer_spec(wu, 0), _layer_spec(wc, layer),
                     _layer_spec(bc, layer), _layer_spec(wd, 0), init_spec]),
        out_specs=[row_spec(d), tail_spec],
        out_shape=[jax.ShapeDtypeStruct((t, d), F32), tail_shape],
        scratch_shapes=[pltpu.VMEM((SUBLANES, 2 * d_ff), F32), pltpu.VMEM((tm, d_ff), BF16)],
        compiler_params=_params(1), name="ffn",
    )(x2d, att, ml, ng, wo, wu, wc, bc, wd, init)


def _bias_rows(rel_table_l):
    n_far = ATT_WINDOW - REL_CLIP + 1
    far = jnp.broadcast_to(rel_table_l[:, 2 * REL_CLIP:], (N_HEADS_A, n_far))
    lo = ATT_WINDOW + REL_CLIP - (ATT_BAND - 1)
    near = rel_table_l[:, lo:2 * REL_CLIP][:, ::-1]
    wrap = jnp.broadcast_to(rel_table_l[:, 2 * REL_CLIP:], (N_HEADS_A, BIAS_W - ATT_BAND))
    return jnp.concatenate([far, near, wrap], axis=1)


def _pad_seq(a, seq_pad):
    return a if a.shape[1] == seq_pad else jnp.pad(a, ((0, 0), (0, seq_pad - a.shape[1]), (0, 0)))


def _pad_cols(a, nb, seq, seq_pad):
    if seq == seq_pad:
        return a
    return jnp.pad(a.reshape(-1, nb, seq), ((0, 0), (0, 0), (0, seq_pad - seq))).reshape(a.shape[0], -1)


def _layer(x, k_cache, v_cache, c0, n0, m0, conv_buf, w):
    nb, seq, d = x.shape
    x2d = x.reshape(nb * seq, d)
    blk = MLSTM_BLOCK if seq % MLSTM_BLOCK == 0 else LANES
    seg = min(blk, seq)
    l = w["layer"]
    qa, va, kb, ft, gcb, gcx = _inproj(x2d, l, w["norm_g"], w["w_in"], w["w_ft"], w["gate_bias"], seg)

    keep = min(ATT_WINDOW, seq) if k_cache is None else seq
    k_tail = jnp.stack([ft[:WIDTH_A, (b + 1) * seq - keep:(b + 1) * seq] for b in range(nb)])
    if k_cache is None:
        att, *w["ffn_w"] = _attn_prompt(qa, ft, va, l, w["gext"], w["g_att"], nb,
                                        (w["w_out"], w["w_up"], w["w_down"]))
    else:
        att = _attn_step(qa, k_tail, va, k_cache, v_cache, l, w["gext"], w["g_att"])
    new_k = k_tail.reshape(nb, N_HEADS_A, HEAD_DIM_A, keep).astype(F32).swapaxes(-1, -2)
    new_v = va.reshape(nb, seq, WIDTH_A)[:, seq - keep:].reshape(nb, keep, N_HEADS_A, HEAD_DIM_A)
    new_v = new_v.transpose(0, 2, 1, 3).astype(F32)

    seq_pad = -(-seq // blk) * blk
    m0b = jnp.broadcast_to(m0.reshape(-1, 1), (nb * N_HEADS_B, LANES))
    r3 = lambda a: _pad_seq(a.reshape(nb, seq, -1), seq_pad)
    hb, c1, n1, m_b = _mlstm(r3(kb), _pad_cols(ft, nb, seq, seq_pad), _pad_cols(gcb, nb, seq, seq_pad), r3(gcx),
                             l, w["g_mlstm"], c0, n0[:, :, None, :], m0b, blk, seq - (seq_pad - blk))
    hb = hb[:, :seq].reshape(nb * seq, WIDTH_B)
    n1, m1 = n1[:, :, 0, :], m_b[:, 0].reshape(nb, N_HEADS_B)

    tm = min(FFN_ROWS, nb * seq)
    if seq < tm:
        init = jnp.pad(conv_buf, ((0, 0), (0, seq - (CONV_W - 1)), (0, 0))).reshape(nb * seq, -1)
        y, u = _ffn(x2d, att, hb, l, w["norm_g"], w["w_conv"], w["b_conv"], *w["ffn_w"], init, seq)
        new_buf = u.reshape(nb, seq, -1)[:, seq - (CONV_W - 1):]
    else:
        y, new_buf = _ffn(x2d, att, hb, l, w["norm_g"], w["w_conv"], w["b_conv"], *w["ffn_w"], conv_buf, seq)
    return y.reshape(nb, seq, d), new_k, new_v, c1, n1, m1, new_buf


def _stacked_weights(norm_g, w_in, b_i, b_f, rel_table, g_att, g_mlstm, w_out, w_up, w_conv, b_conv, w_down):
    wa, wb = WIDTH_A, WIDTH_B
    assert wa == wb
    col = lambda k: slice(k * wa, (k + 1) * wa)
    wi = w_in.astype(BF16)
    wg = w_in[..., 7 * wa:]
    wg_hi = wg.astype(BF16)
    wg_lo = (wg - wg_hi.astype(F32)).astype(BF16)
    w_ft = jnp.concatenate([wi[..., col(k)] for k in (1, 3, 5, 6)] + [wg_hi, wg_lo], axis=-1).swapaxes(1, 2)
    return dict(
        norm_g=norm_g, w_in=wi, w_ft=w_ft, gate_bias=jnp.concatenate([b_i, b_f], axis=-1)[..., None],
        gext=jax.vmap(_bias_rows)(rel_table), g_att=g_att[:, None, :], g_mlstm=g_mlstm[:, None, :],
        w_out=w_out, w_up=w_up, w_conv=w_conv, b_conv=b_conv[:, None, :], w_down=w_down)


def kernel(x_prompt, x_sample, cache_k_att, cache_v_att, state_mlstm_c, state_mlstm_n, state_mlstm_m,
           cache_ffn_conv, norm_g, w_in, b_i, b_f, rel_table, g_att, g_mlstm, w_out, w_up, w_conv,
           b_conv, w_down):
    depth = w_in.shape[0]
    bp = x_prompt.shape[0]
    d_ff2 = w_up.shape[2]
    dh = HEAD_DIM_B
    c_zero = jnp.zeros((bp, N_HEADS_B, dh, dh), F32)
    n_zero = jnp.zeros((bp, N_HEADS_B, dh), F32)
    m_zero = jnp.zeros((bp, N_HEADS_B), F32)
    buf_zero = jnp.zeros((bp, CONV_W - 1, d_ff2), F32)
    xp, xs = x_prompt, x_sample
    ck_t = cache_k_att.swapaxes(-1, -2)
    cv_t = cache_v_att.swapaxes(-1, -2)
    outs_p, outs_s = [], []
    stacks = _stacked_weights(norm_g, w_in, b_i, b_f, rel_table, g_att, g_mlstm, w_out, w_up, w_conv, b_conv,
                              w_down)
    for l in range(depth):
        w = dict(stacks, layer=l)
        xp, *st_p = _layer(xp, None, None, c_zero, n_zero, m_zero, buf_zero, w)
        xs, *st_s = _layer(xs, ck_t, cv_t, state_mlstm_c[l], state_mlstm_n[l],
                           state_mlstm_m[l], cache_ffn_conv[l], w)
        outs_p.append(st_p)
        outs_s.append(st_s)
    stack = lambda outs, i: jnp.stack([o[i] for o in outs])
    return (xp, xs) + tuple(stack(outs_p, i) for i in range(6)) + tuple(stack(outs_s, i) for i in range(6))
```
